```python
import math
import jax
import jax.numpy as jnp
from jax import lax
import numpy as np

D_MODEL = 1024
BATCH = 8
SEQ = 2048
DEPTH = 2
DEC_BATCH = 128
DEC_SEQ = 4
PAST_LEN = 16384
PAGE_SIZE = 128

N_MEM = 256
EPS = 1e-6
N_BRANCH = 3
BRANCH_W = D_MODEL // 2
POOL_WINDOWS = (2, 4, 8, 16)
POOL_GROUPS = len(POOL_WINDOWS)
POOL_GW = BRANCH_W // POOL_GROUPS
POOL_BUF = max(POOL_WINDOWS) - 1
DN_HEADS = 4
DN_HD = BRANCH_W // DN_HEADS
DN_CONV = 4
DN_CHUNK = 64
XA_HEADS = 4
XA_HD = BRANCH_W // XA_HEADS
PEER_HEADS = 8
PEER_NKEYS = 128
PEER_N = PEER_NKEYS * PEER_NKEYS
PEER_QD = 256
PEER_HALF = PEER_QD // 2
PEER_TOPK = 16
PEER_BLOCK = 128

OFF_POOL = 0
OFF_Q = OFF_POOL + BRANCH_W
OFF_Z = OFF_Q + 3 * BRANCH_W
OFF_BETA = OFF_Z + BRANCH_W
OFF_ALPHA = OFF_BETA + DN_HEADS
OFF_XQ = OFF_ALPHA + DN_HEADS
OFF_GATE = OFF_XQ + BRANCH_W
IN_COLS = OFF_GATE + N_BRANCH * D_MODEL

kernel_name = 'hybrid_pool_delta_peer_step'


def rmsnorm(x, g):
    xf = x.astype(jnp.float32)
    y = xf * lax.rsqrt(jnp.mean(xf * xf, axis=-1, keepdims=True) + EPS)
    return (y * g.astype(jnp.float32)).astype(x.dtype)


def l2norm(a):
    return a * lax.rsqrt(jnp.sum(a * a, axis=-1, keepdims=True) + EPS)


def pool_mixer(u, buf, start, w_grp, scale):
    B, T, _ = u.shape
    ext = jnp.concatenate([buf.astype(u.dtype), u], axis=1)
    cs = jnp.cumsum(ext.astype(jnp.float32), axis=1)
    cs = jnp.pad(cs, ((0, 0), (1, 0), (0, 0)))
    hi = cs[:, POOL_BUF + 1:POOL_BUF + 1 + T]
    pos = start + jnp.arange(T)
    means = []
    for gi, win in enumerate(POOL_WINDOWS):
        sl = slice(gi * POOL_GW, (gi + 1) * POOL_GW)
        lo = cs[:, POOL_BUF + 1 - win:POOL_BUF + 1 - win + T, sl]
        cnt = jnp.minimum(pos + 1, win).astype(jnp.float32)
        means.append((hi[..., sl] - lo) / cnt[None, :, None])
    d = (jnp.concatenate(means, axis=-1) - u.astype(jnp.float32)).astype(u.dtype)
    d = d.reshape(B, T, POOL_GROUPS, POOL_GW)
    y = jnp.einsum('btgc,gce->btge', d, w_grp).reshape(B, T, BRANCH_W) * scale
    return y, ext[:, -POOL_BUF:]


def short_conv(x, buf, w):
    T = x.shape[1]
    ext = jnp.concatenate([buf.astype(x.dtype), x], axis=1)
    y = ext[:, 0:T] * w[0]
    for j in range(1, DN_CONV):
        y = y + ext[:, j:j + T] * w[j]
    return jax.nn.silu(y), ext[:, -(DN_CONV - 1):]


def gated_delta(q, k, v, beta, g, s0):
    B, T, H, d = q.shape
    C = min(DN_CHUNK, T)
    pad = (-T) % C
    N = (T + pad) // C

    def to_chunks(a):
        a = jnp.pad(a, [(0, 0), (0, pad)] + [(0, 0)] * (a.ndim - 2))
        a = a.reshape((B, N, C) + a.shape[2:])
        return jnp.moveaxis(a, 3, 1)

    q, k, v, beta, g = (to_chunks(a) for a in (q, k, v, beta, g))
    q = q * (d ** -0.5)
    kb = k * beta[..., None]
    vb = v * beta[..., None]
    gc = jnp.cumsum(g, axis=-1)
    idx = jnp.arange(C)
    tril = idx[:, None] >= idx[None, :]
    strict = idx[:, None] > idx[None, :]
    diff = gc[..., :, None] - gc[..., None, :]
    decay = jnp.where(tril, jnp.exp(jnp.where(tril, diff, 0.0)), 0.0)
    A = jnp.where(strict, jnp.einsum('bhncd,bhnmd->bhncm', kb, k) * decay, 0.0)
    M = A + jnp.eye(C, dtype=A.dtype)
    u = lax.linalg.triangular_solve(M, vb, left_side=True, lower=True, unit_diagonal=True)
    w = lax.linalg.triangular_solve(M, kb * jnp.exp(gc)[..., None], left_side=True,
                                    lower=True, unit_diagonal=True)
    attn = jnp.where(tril, jnp.einsum('bhncd,bhnmd->bhncm', q, k) * decay, 0.0)
    qg = q * jnp.exp(gc)[..., None]
    glast = gc[..., -1]
    kd = k * jnp.exp(glast[..., None] - gc)[..., None]
    xs = tuple(jnp.moveaxis(a, 2, 0) for a in (u, w, attn, qg, kd, jnp.exp(glast)))

    def step(S, inp):
        u_n, w_n, at_n, qg_n, kd_n, gl_n = inp
        v_new = u_n - jnp.einsum('bhcd,bhde->bhce', w_n, S)
        o = jnp.einsum('bhcd,bhde->bhce', qg_n, S) + jnp.einsum('bhcm,bhme->bhce', at_n, v_new)
        S = S * gl_n[..., None, None] + jnp.einsum('bhcd,bhce->bhde', kd_n, v_new)
        return S, o

    s_fin, o = lax.scan(step, s0, xs)
    o = jnp.moveaxis(o, 0, 2).reshape(B, H, N * C, d)[:, :, :T]
    return jnp.transpose(o, (0, 2, 1, 3)), s_fin


def mem_kv(mem, g, w):
    B = mem.shape[0]
    kv = rmsnorm(mem, g) @ w
    k = kv[..., :BRANCH_W].reshape(B, -1, XA_HEADS, XA_HD)
    v = kv[..., BRANCH_W:].reshape(B, -1, XA_HEADS, XA_HD)
    return k, v


def mem_attend(xq, mk, mv):
    B, T, _ = xq.shape
    q = xq.reshape(B, T, XA_HEADS, XA_HD)
    s = jnp.einsum('bthd,bmhd->bhtm', q, mk.astype(xq.dtype)).astype(jnp.float32) * (XA_HD ** -0.5)
    p = jax.nn.softmax(s, axis=-1).astype(xq.dtype)
    o = jnp.einsum('bhtm,bmhd->bthd', p, mv.astype(xq.dtype))
    return o.reshape(B, T, BRANCH_W)


def mixer_block(x, start, mem_k, mem_v, pool_buf, conv_buf, s0, g_mix, w_in, w_conv,
                a_log, dt_bias, g_dn_out, w_pool_grp, pool_scale, w_branch, w_o):
    B, T, _ = x.shape
    f32 = jnp.float32
    h = rmsnorm(x, g_mix)
    proj = h @ w_in
    u_pool = proj[..., OFF_POOL:OFF_Q]
    qkv = proj[..., OFF_Q:OFF_Z]
    z = proj[..., OFF_Z:OFF_BETA]
    beta_raw = proj[..., OFF_BETA:OFF_ALPHA]
    alpha_raw = proj[..., OFF_ALPHA:OFF_XQ]
    xq = proj[..., OFF_XQ:OFF_GATE]
    gate_raw = proj[..., OFF_GATE:]
    y_pool, new_pool = pool_mixer(u_pool, pool_buf, start, w_pool_grp, pool_scale)
    qkv_c, new_conv = short_conv(qkv, conv_buf, w_conv)
    qkv_c = qkv_c.astype(f32).reshape(B, T, 3, DN_HEADS, DN_HD)
    q = l2norm(qkv_c[:, :, 0])
    k = l2norm(qkv_c[:, :, 1])
    v = qkv_c[:, :, 2]
    beta = jax.nn.sigmoid(beta_raw.astype(f32))
    g = -jnp.exp(a_log.astype(f32)) * jax.nn.softplus(alpha_raw.astype(f32) + dt_bias.astype(f32))
    o, s_new = gated_delta(q, k, v, beta, g, s0.astype(f32))
    o = o * lax.rsqrt(jnp.mean(o * o, axis=-1, keepdims=True) + EPS) * g_dn_out.astype(f32)
    o = o * jax.nn.silu(z.astype(f32).reshape(B, T, DN_HEADS, DN_HD))
    y_dn = o.reshape(B, T, BRANCH_W)
    y_mem = mem_attend(xq, mem_k, mem_v)
    br = jnp.stack([y_pool.astype(x.dtype), y_dn.astype(x.dtype), y_mem.astype(x.dtype)], axis=2)
    bproj = jnp.einsum('btnc,ncd->btnd', br, w_branch)
    gates = jax.nn.sigmoid(gate_raw.reshape(B, T, N_BRANCH, D_MODEL))
    merged = jnp.sum(gates * bproj, axis=2)
    return x + merged @ w_o, new_pool, new_conv, s_new


def peer_ffn(h, w_q, subkeys, U, V):
    B, T, D = h.shape
    q = (h @ w_q).reshape(B, T, PEER_HEADS, 2, PEER_HALF)
    s = jnp.einsum('bthpc,hpkc->bthpk', q, subkeys).astype(jnp.float32)
    s_top, i_top = lax.top_k(s, PEER_TOPK)
    ncand = PEER_TOPK * PEER_TOPK
    cand = (s_top[..., 0, :, None] + s_top[..., 1, None, :]).reshape(B, T, PEER_HEADS, ncand)
    cidx = (i_top[..., 0, :, None] * PEER_NKEYS + i_top[..., 1, None, :]).reshape(B, T, PEER_HEADS, ncand)
    best, pos = lax.top_k(cand, PEER_TOPK)
    idx = jnp.take_along_axis(cidx, pos, axis=-1)
    gate = jax.nn.softmax(best, axis=-1).astype(h.dtype)
    n = B * T
    blk = min(PEER_BLOCK, n)
    pad = (-n) % blk
    nb = (n + pad) // blk
    xt = jnp.pad(h.reshape(n, D), ((0, pad), (0, 0))).reshape(nb, blk, D)
    it = jnp.pad(idx.reshape(n, PEER_HEADS, PEER_TOPK), ((0, pad), (0, 0), (0, 0))).reshape(nb, blk, PEER_HEADS, PEER_TOPK)
    gt = jnp.pad(gate.reshape(n, PEER_HEADS, PEER_TOPK), ((0, pad), (0, 0), (0, 0))).reshape(nb, blk, PEER_HEADS, PEER_TOPK)

    def expert_block(args):
        xb, ib, gb = args
        a = jnp.einsum('td,thed->the', xb, jnp.take(U, ib, axis=0))
        wgt = gb * jax.nn.gelu(a)
        return jnp.einsum('the,thed->td', wgt, jnp.take(V, ib, axis=0))

    out = lax.map(expert_block, (xt, it, gt))
    return out.reshape(nb * blk, D)[:n].reshape(B, T, D)


def setup_inputs(seed: int = 0) -> dict:
    key = jax.random.key(seed)
    ks = iter(jax.random.split(key, 32))

    def nrm(shape, scale=1.0):
        return jax.random.normal(next(ks), shape, jnp.float32) * scale

    def gain(shape):
        return 1.0 + nrm(shape, 0.02)

    x_prompt = nrm((BATCH, SEQ, D_MODEL))
    x_sample = nrm((DEC_BATCH, DEC_SEQ, D_MODEL))
    state_pool = nrm((DEPTH, DEC_BATCH, POOL_BUF, BRANCH_W))
    state_conv = nrm((DEPTH, DEC_BATCH, DN_CONV - 1, 3 * BRANCH_W))
    state_delta = nrm((DEPTH, DEC_BATCH, DN_HEADS, DN_HD, DN_HD), 0.05)
    cache_mem_k = nrm((DEPTH, DEC_BATCH, N_MEM, XA_HEADS, XA_HD))
    cache_mem_v = nrm((DEPTH, DEC_BATCH, N_MEM, XA_HEADS, XA_HD))
    mem_prompt = nrm((BATCH, N_MEM, D_MODEL))
    g_mix = gain((DEPTH, D_MODEL))
    w_in = nrm((DEPTH, D_MODEL, IN_COLS), D_MODEL ** -0.5)
    w_conv = nrm((DEPTH, DN_CONV, 3 * BRANCH_W), DN_CONV ** -0.5)
    a_log = jnp.log(jax.random.uniform(next(ks), (DEPTH, DN_HEADS), jnp.float32, minval=1.0, maxval=16.0))
    dt = jnp.exp(jax.random.uniform(next(ks), (DEPTH, DN_HEADS), jnp.float32,
                                    minval=math.log(1e-3), maxval=math.log(1e-1)))
    dt_bias = dt + jnp.log(-jnp.expm1(-dt))
    g_dn_out = gain((DEPTH, DN_HD))
    w_pool_grp = nrm((DEPTH, POOL_GROUPS, POOL_GW, POOL_GW), POOL_GW ** -0.5)
    pool_scale = gain((DEPTH, BRANCH_W))
    g_mem = gain((DEPTH, D_MODEL))
    w_mem_kv = nrm((DEPTH, D_MODEL, 2 * BRANCH_W), D_MODEL ** -0.5)
    w_branch = nrm((DEPTH, N_BRANCH, BRANCH_W, D_MODEL), BRANCH_W ** -0.5)
    w_o = nrm((DEPTH, D_MODEL, D_MODEL), D_MODEL ** -0.5)
    g_ffn = gain((DEPTH, D_MODEL))
    w_peer_q = nrm((DEPTH, D_MODEL, PEER_HEADS * PEER_QD), D_MODEL ** -0.5)
    peer_subkeys = nrm((DEPTH, PEER_HEADS, 2, PEER_NKEYS, PEER_HALF), PEER_HALF ** -0.5)
    peer_u = nrm((DEPTH, PEER_N, D_MODEL), D_MODEL ** -0.5)
    peer_v = nrm((DEPTH, PEER_N, D_MODEL), (PEER_HEADS * PEER_TOPK) ** -0.5)
    g_final = gain((D_MODEL,))
    return {'x_prompt': x_prompt, 'x_sample': x_sample, 'state_pool': state_pool,
            'state_conv': state_conv, 'state_delta': state_delta, 'cache_mem_k': cache_mem_k,
            'cache_mem_v': cache_mem_v, 'mem_prompt': mem_prompt, 'g_mix': g_mix, 'w_in': w_in,
            'w_conv': w_conv, 'a_log': a_log, 'dt_bias': dt_bias, 'g_dn_out': g_dn_out,
            'w_pool_grp': w_pool_grp, 'pool_scale': pool_scale, 'g_mem': g_mem,
            'w_mem_kv': w_mem_kv, 'w_branch': w_branch, 'w_o': w_o, 'g_ffn': g_ffn,
            'w_peer_q': w_peer_q, 'peer_subkeys': peer_subkeys, 'peer_u': peer_u,
            'peer_v': peer_v, 'g_final': g_final}


def reference(x_prompt, x_sample, state_pool, state_conv, state_delta, cache_mem_k, cache_mem_v,
              mem_prompt, g_mix, w_in, w_conv, a_log, dt_bias, g_dn_out, w_pool_grp, pool_scale,
              g_mem, w_mem_kv, w_branch, w_o, g_ffn, w_peer_q, peer_subkeys, peer_u, peer_v, g_final):
    Bp = x_prompt.shape[0]
    xp = x_prompt
    xs = x_sample
    pool_p, conv_p, delta_p, mk_p, mv_p = [], [], [], [], []
    pool_s, conv_s, delta_s = [], [], []
    for l in range(DEPTH):
        mix_w = (g_mix[l], w_in[l], w_conv[l], a_log[l], dt_bias[l], g_dn_out[l],
                 w_pool_grp[l], pool_scale[l], w_branch[l], w_o[l])
        mk, mv = mem_kv(mem_prompt, g_mem[l], w_mem_kv[l])
        xp, pb, cb, sp = mixer_block(
            xp, 0, mk, mv,
            jnp.zeros((Bp, POOL_BUF, BRANCH_W), xp.dtype),
            jnp.zeros((Bp, DN_CONV - 1, 3 * BRANCH_W), xp.dtype),
            jnp.zeros((Bp, DN_HEADS, DN_HD, DN_HD), jnp.float32),
            *mix_w)
        xp = xp + peer_ffn(rmsnorm(xp, g_ffn[l]), w_peer_q[l], peer_subkeys[l], peer_u[l], peer_v[l])
        pool_p.append(pb)
        conv_p.append(cb)
        delta_p.append(sp.astype(xp.dtype))
        mk_p.append(mk)
        mv_p.append(mv)
        xs, pb, cb, ss = mixer_block(
            xs, PAST_LEN, cache_mem_k[l], cache_mem_v[l],
            state_pool[l], state_conv[l], state_delta[l], *mix_w)
        xs = xs + peer_ffn(rmsnorm(xs, g_ffn[l]), w_peer_q[l], peer_subkeys[l], peer_u[l], peer_v[l])
        pool_s.append(pb.astype(state_pool.dtype))
        conv_s.append(cb.astype(state_conv.dtype))
        delta_s.append(ss.astype(state_delta.dtype))
    y_prompt = rmsnorm(xp, g_final)
    y_sample = rmsnorm(xs, g_final)
    new_pool_p = jnp.stack(pool_p, axis=0)
    new_conv_p = jnp.stack(conv_p, axis=0)
    new_delta_p = jnp.stack(delta_p, axis=0)
    mem_k_p = jnp.stack(mk_p, axis=0)
    mem_v_p = jnp.stack(mv_p, axis=0)
    new_pool_s = jnp.stack(pool_s, axis=0)
    new_conv_s = jnp.stack(conv_s, axis=0)
    new_delta_s = jnp.stack(delta_s, axis=0)
    return (y_prompt, y_sample, new_pool_p, new_conv_p, new_delta_p, mem_k_p, mem_v_p,
            new_pool_s, new_conv_s, new_delta_s)
```

```python
import functools
import math

import jax
import jax.numpy as jnp
from jax import lax
from jax.experimental import pallas as pl
from jax.experimental.pallas import tpu as pltpu

F32 = jnp.float32
BF16 = jnp.bfloat16
HIGHEST = lax.Precision.HIGHEST

D_MODEL = 1024
DEPTH = 2
PAST_LEN = 16384
N_MEM = 256
EPS = 1e-6
BRANCH_W = D_MODEL // 2
POOL_WINDOWS = (2, 4, 8, 16)
POOL_GW = BRANCH_W // len(POOL_WINDOWS)
POOL_BUF = max(POOL_WINDOWS) - 1
DN_HEADS = 4
DN_HD = BRANCH_W // DN_HEADS
DN_CONV = 4
DN_CHUNK = 64
XA_HEADS = 4
XA_HD = BRANCH_W // XA_HEADS
PEER_HEADS = 8
PEER_NKEYS = 128
PEER_N = PEER_NKEYS * PEER_NKEYS
PEER_HALF = 128
PEER_TOPK = 16

OFF_Q = BRANCH_W
OFF_Z = OFF_Q + 3 * BRANCH_W
OFF_BETA = OFF_Z + BRANCH_W
OFF_ALPHA = OFF_BETA + DN_HEADS
OFF_XQ = OFF_ALPHA + DN_HEADS
OFF_GATE = OFF_XQ + BRANCH_W

LANES = 128
SUBLANES = 8
POOL_HIST = 16
CONV_HIST = SUBLANES
VMEM_LIMIT = 56 * 1024 * 1024


def _cparams(sem):
    return pltpu.CompilerParams(dimension_semantics=sem, vmem_limit_bytes=VMEM_LIMIT)


def _sigmoid(x):
    return 1.0 / (1.0 + jnp.exp(-x))


def _dot(a, b, precision=None):
    return jnp.dot(a, b, preferred_element_type=F32, precision=precision)


def _dot_nt(a, b, precision=None):
    return lax.dot_general(a, b, (((1,), (1,)), ((), ())), preferred_element_type=F32,
                           precision=precision)


def _dot_tn(a, b, precision=None):
    return lax.dot_general(a, b, (((0,), (0,)), ((), ())), preferred_element_type=F32,
                           precision=precision)


def _rms(x, g):
    return x * lax.rsqrt(jnp.mean(x * x, axis=-1, keepdims=True) + EPS) * g


def _norm_matmul_kernel(x_ref, g_ref, *refs, n_w, col_chunk):
    w_refs = refs[:n_w]
    o_refs = refs[n_w:2 * n_w]
    hb = _rms(x_ref[...], g_ref[...]).astype(BF16)
    for w_ref, o_ref in zip(w_refs, o_refs):
        cols = w_ref.shape[1]
        for c0 in range(0, cols, col_chunk):
            c1 = min(cols, c0 + col_chunk)
            o_ref[:, c0:c1] = _dot(hb, w_ref[:, c0:c1])


def norm_matmul(x, g, ws, *, tm=256):
    n, d = x.shape
    tm = min(tm, n)
    in_specs = [pl.BlockSpec((tm, d), lambda i: (i, 0)), pl.BlockSpec((1, d), lambda i: (0, 0))]
    in_specs += [pl.BlockSpec(w.shape, lambda i: (0, 0)) for w in ws]
    out_specs = [pl.BlockSpec((tm, w.shape[1]), lambda i: (i, 0)) for w in ws]
    out_shape = [jax.ShapeDtypeStruct((n, w.shape[1]), F32) for w in ws]
    return pl.pallas_call(
        functools.partial(_norm_matmul_kernel, n_w=len(ws), col_chunk=512),
        grid=(n // tm,), in_specs=in_specs, out_specs=out_specs, out_shape=out_shape,
        compiler_params=_cparams(("parallel",)), name="norm_matmul",
    )(x, g.reshape(1, d), *ws)


def _peer_query_kernel(x_ref, g_ref, w_ref, q_ref, h_ref):
    hb = _rms(x_ref[...], g_ref[...]).astype(BF16)
    h_ref[...] = hb
    for hd in range(PEER_HEADS):
        q = _dot(hb, w_ref[:, hd * 2 * PEER_HALF:(hd + 1) * 2 * PEER_HALF])
        q_ref[2 * hd] = q[:, :PEER_HALF]
        q_ref[2 * hd + 1] = q[:, PEER_HALF:]


def peer_query(x, g, w, *, tm=256):
    n, d = x.shape
    tm = min(tm, n)
    nq = 2 * PEER_HEADS
    return pl.pallas_call(
        _peer_query_kernel, grid=(n // tm,),
        in_specs=[pl.BlockSpec((tm, d), lambda i: (i, 0)), pl.BlockSpec((1, d), lambda i: (0, 0)),
                  pl.BlockSpec(w.shape, lambda i: (0, 0))],
        out_specs=[pl.BlockSpec((nq, tm, PEER_HALF), lambda i: (0, i, 0)),
                   pl.BlockSpec((tm, d), lambda i: (i, 0))],
        out_shape=[jax.ShapeDtypeStruct((nq, n, PEER_HALF), F32), jax.ShapeDtypeStruct((n, d), BF16)],
        compiler_params=_cparams(("parallel",)), name="peer_query",
    )(x, g.reshape(1, d), w)


def _pool_kernel(u_ref, buf_ref, w_ref, scale_ref, y_ref, ext_ref, *, tt, start):
    ti = pl.program_id(1)

    @pl.when(ti == 0)
    def _():
        ext_ref[0:POOL_HIST, :] = buf_ref[0]

    ext_ref[POOL_HIST:POOL_HIST + tt, :] = u_ref[0]
    pos = start + ti * tt + lax.broadcasted_iota(jnp.int32, (tt, 1), 0)
    for gi, win in enumerate(POOL_WINDOWS):
        sl = slice(gi * POOL_GW, (gi + 1) * POOL_GW)
        u = ext_ref[POOL_HIST:POOL_HIST + tt, sl]
        s = u
        for k in range(1, win):
            s = s + ext_ref[POOL_HIST - k:POOL_HIST - k + tt, sl]
        cnt = jnp.minimum(pos + 1, win).astype(F32)
        d = s / cnt - u
        y_ref[0, :, sl] = _dot(d.astype(BF16), w_ref[gi]) * scale_ref[:, sl]
    ext_ref[0:POOL_HIST, :] = ext_ref[tt:tt + POOL_HIST, :]


def pool_mixer(u, buf, w_grp, scale, start, *, tt=512):
    b, t, c = u.shape
    tt = min(tt, t)
    hist = jnp.pad(buf, ((0, 0), (POOL_HIST - POOL_BUF, 0), (0, 0)))
    return pl.pallas_call(
        functools.partial(_pool_kernel, tt=tt, start=start), grid=(b, t // tt),
        in_specs=[pl.BlockSpec((1, tt, c), lambda i, j: (i, j, 0)),
                  pl.BlockSpec((1, POOL_HIST, c), lambda i, j: (i, 0, 0)),
                  pl.BlockSpec(w_grp.shape, lambda i, j: (0, 0, 0)),
                  pl.BlockSpec((1, c), lambda i, j: (0, 0))],
        out_specs=pl.BlockSpec((1, tt, c), lambda i, j: (i, j, 0)),
        out_shape=jax.ShapeDtypeStruct((b, t, c), F32),
        scratch_shapes=[pltpu.VMEM((POOL_HIST + tt, c), F32)],
        compiler_params=_cparams(("parallel", "arbitrary")), name="pool_mixer",
    )(u, hist, w_grp, scale.reshape(1, c))


def _softplus(x):
    return jnp.maximum(x, 0.0) + jnp.log(1.0 + jnp.exp(-jnp.abs(x)))


def _conv_kernel(x_ref, buf_ref, w_ref, ba_ref, alog_ref, dtb_ref, q_ref, k_ref, v_ref, bg_ref,
                 ext_ref, *, tt):
    ti = pl.program_id(1)

    @pl.when(ti == 0)
    def _():
        ext_ref[0:CONV_HIST, :] = buf_ref[0]

    ext_ref[CONV_HIST:CONV_HIST + tt, :] = x_ref[0]
    base = CONV_HIST - (DN_CONV - 1)
    y = ext_ref[base:base + tt, :] * w_ref[0:1, :]
    for j in range(1, DN_CONV):
        y = y + ext_ref[base + j:base + j + tt, :] * w_ref[j:j + 1, :]
    y = y * _sigmoid(y)
    for part, o_ref in enumerate((q_ref, k_ref, v_ref)):
        for h in range(DN_HEADS):
            a = y[:, part * BRANCH_W + h * DN_HD:part * BRANCH_W + (h + 1) * DN_HD]
            if part < 2:
                a = a * lax.rsqrt(jnp.sum(a * a, axis=-1, keepdims=True) + EPS)
            o_ref[0, :, h * DN_HD:(h + 1) * DN_HD] = a
    ba = ba_ref[0]
    beta = _sigmoid(ba)
    g = -jnp.exp(alog_ref[...]) * _softplus(ba + dtb_ref[...])
    lane = lax.broadcasted_iota(jnp.int32, ba.shape, 1)
    bg_ref[0] = jnp.where(lane < DN_HEADS, beta, g)
    ext_ref[0:CONV_HIST, :] = ext_ref[tt:tt + CONV_HIST, :]


def conv_prep(qkv, buf, w_conv, ba, a_log, dt_bias, *, tt=256):
    b, t, c = qkv.shape
    tt = min(tt, t)
    hist = jnp.pad(buf, ((0, 0), (CONV_HIST - (DN_CONV - 1), 0), (0, 0)))
    pad = jnp.zeros((LANES - 2 * DN_HEADS,), F32)
    alog_v = jnp.concatenate([jnp.zeros((DN_HEADS,), F32), a_log, pad]).reshape(1, LANES)
    dtb_v = jnp.concatenate([jnp.zeros((DN_HEADS,), F32), dt_bias, pad]).reshape(1, LANES)
    seq = lambda w: pl.BlockSpec((1, tt, w), lambda i, j: (i, j, 0))
    return pl.pallas_call(
        functools.partial(_conv_kernel, tt=tt), grid=(b, t // tt),
        in_specs=[seq(c), pl.BlockSpec((1, CONV_HIST, c), lambda i, j: (i, 0, 0)),
                  pl.BlockSpec(w_conv.shape, lambda i, j: (0, 0)), seq(LANES),
                  pl.BlockSpec((1, LANES), lambda i, j: (0, 0)),
                  pl.BlockSpec((1, LANES), lambda i, j: (0, 0))],
        out_specs=[seq(BRANCH_W), seq(BRANCH_W), seq(BRANCH_W), seq(LANES)],
        out_shape=[jax.ShapeDtypeStruct((b, t, BRANCH_W), F32)] * 3
        + [jax.ShapeDtypeStruct((b, t, LANES), F32)],
        scratch_shapes=[pltpu.VMEM((CONV_HIST + tt, c), F32)],
        compiler_params=_cparams(("parallel", "arbitrary")), name="conv_prep",
    )(qkv, hist, w_conv, ba, alog_v, dtb_v)


def _delta_kernel(q_ref, k_ref, v_ref, bg_ref, bgt_ref, z_ref, s0_ref, gdn_ref, y_ref, sfin_ref,
                  s_ref, *, c):
    n = pl.program_id(1)

    @pl.when(n == 0)
    def _():
        s_ref[...] = s0_ref[0]

    row = lax.broadcasted_iota(jnp.int32, (c, c), 0)
    col = lax.broadcasted_iota(jnp.int32, (c, c), 1)
    tril = row >= col
    strict = row > col
    ltri = tril.astype(F32)
    eye = (row == col).astype(F32)
    bg = bg_ref[0]
    gc_all = _dot(ltri, bg, HIGHEST)
    gct_all = _dot_nt(bgt_ref[0, 0], ltri, HIGHEST)
    n_double = int(round(math.log2(c))) - 1
    for h in range(DN_HEADS):
        sl = slice(h * DN_HD, (h + 1) * DN_HD)
        q = q_ref[0, :, sl] * (DN_HD ** -0.5)
        k = k_ref[0, :, sl]
        v = v_ref[0, :, sl]
        beta = bg[:, h:h + 1]
        gc = gc_all[:, DN_HEADS + h:DN_HEADS + h + 1]
        gcr = gct_all[DN_HEADS + h:DN_HEADS + h + 1, :]
        decay = jnp.where(tril, jnp.exp(jnp.where(tril, gc - gcr, 0.0)), 0.0)
        kb = k * beta
        vb = v * beta
        k16 = k.astype(BF16)
        a = jnp.where(strict, _dot_nt(kb.astype(BF16), k16) * decay, 0.0)
        p = -a
        tinv = eye + p
        for _ in range(n_double):
            p = _dot(p, p, HIGHEST)
            tinv = tinv + _dot(tinv, p, HIGHEST)
        egc = jnp.exp(gc)
        u = _dot(tinv, vb, HIGHEST)
        w = _dot(tinv, kb * egc, HIGHEST)
        attn = jnp.where(tril, _dot_nt(q.astype(BF16), k16) * decay, 0.0)
        qg = q * egc
        gl = gc[c - 1:c, :]
        kd = k * jnp.exp(gl - gc)
        s = s_ref[h]
        s16 = s.astype(BF16)
        v_new = u - _dot(w.astype(BF16), s16)
        o = _dot(qg.astype(BF16), s16) + _dot(attn.astype(BF16), v_new.astype(BF16))
        s_ref[h] = s * jnp.exp(gl) + _dot_tn(kd.astype(BF16), v_new.astype(BF16))
        o = o * lax.rsqrt(jnp.mean(o * o, axis=-1, keepdims=True) + EPS) * gdn_ref[...]
        z = z_ref[0, :, sl]
        y_ref[0, :, sl] = o * (z * _sigmoid(z))

    @pl.when(n == pl.num_programs(1) - 1)
    def _():
        sfin_ref[0] = s_ref[...]


def gated_delta(q, k, v, bg, z, s0, g_dn_out, *, c):
    b, t, _ = q.shape
    nc = t // c
    bgt = jnp.swapaxes(bg[..., :SUBLANES].reshape(b, nc, c, SUBLANES), 2, 3)
    seq = lambda w: pl.BlockSpec((1, c, w), lambda i, j: (i, j, 0))
    st = pl.BlockSpec((1, DN_HEADS, DN_HD, DN_HD), lambda i, j: (i, 0, 0, 0))
    return pl.pallas_call(
        functools.partial(_delta_kernel, c=c), grid=(b, nc),
        in_specs=[seq(BRANCH_W), seq(BRANCH_W), seq(BRANCH_W), seq(LANES),
                  pl.BlockSpec((1, 1, SUBLANES, c), lambda i, j: (i, j, 0, 0)), seq(BRANCH_W), st,
                  pl.BlockSpec((1, DN_HD), lambda i, j: (0, 0))],
        out_specs=[seq(BRANCH_W), st],
        out_shape=[jax.ShapeDtypeStruct((b, t, BRANCH_W), F32),
                   jax.ShapeDtypeStruct((b, DN_HEADS, DN_HD, DN_HD), F32)],
        scratch_shapes=[pltpu.VMEM((DN_HEADS, DN_HD, DN_HD), F32)],
        compiler_params=_cparams(("parallel", "arbitrary")), name="gated_delta",
    )(q, k, v, bg, bgt, z, s0, g_dn_out.reshape(1, DN_HD))


def _attn_kernel(q_ref, k_ref, v_ref, o_ref):
    for h in range(XA_HEADS):
        sl = slice(h * XA_HD, (h + 1) * XA_HD)
        q = q_ref[0, :, sl].astype(BF16)
        k = k_ref[0, :, sl].astype(BF16)
        v = v_ref[0, :, sl].astype(BF16)
        s = _dot_nt(q, k) * (XA_HD ** -0.5)
        p = jnp.exp(s - jnp.max(s, axis=-1, keepdims=True))
        p = p / jnp.sum(p, axis=-1, keepdims=True)
        o_ref[0, :, sl] = _dot(p.astype(BF16), v)


def mem_attend(xq, mk, mv, *, tq=512):
    b, t, c = xq.shape
    tq = min(tq, t)
    kv = pl.BlockSpec((1, N_MEM, c), lambda i, j: (i, 0, 0))
    return pl.pallas_call(
        _attn_kernel, grid=(b, t // tq),
        in_specs=[pl.BlockSpec((1, tq, c), lambda i, j: (i, j, 0)), kv, kv],
        out_specs=pl.BlockSpec((1, tq, c), lambda i, j: (i, j, 0)),
        out_shape=jax.ShapeDtypeStruct((b, t, c), F32),
        compiler_params=_cparams(("parallel", "parallel")), name="mem_attend",
    )(xq, mk, mv)


def _merge_kernel(x_ref, yp_ref, yd_ref, ym_ref, gate_ref, wb_ref, wo_ref, o_ref):
    acc = None
    for n, y_ref in enumerate((yp_ref, yd_ref, ym_ref)):
        bp = _dot(y_ref[...].astype(BF16), wb_ref[n])
        t = _sigmoid(gate_ref[:, n * D_MODEL:(n + 1) * D_MODEL]) * bp
        acc = t if acc is None else acc + t
    o_ref[...] = x_ref[...] + _dot(acc.astype(BF16), wo_ref[...])


def merge(x, y_pool, y_dn, y_mem, gate_raw, w_branch, w_o, *, tm=256):
    n, d = x.shape
    tm = min(tm, n)
    row = lambda w: pl.BlockSpec((tm, w), lambda i: (i, 0))
    return pl.pallas_call(
        _merge_kernel, grid=(n // tm,),
        in_specs=[row(d), row(BRANCH_W), row(BRANCH_W), row(BRANCH_W), row(3 * d),
                  pl.BlockSpec(w_branch.shape, lambda i: (0, 0, 0)),
                  pl.BlockSpec(w_o.shape, lambda i: (0, 0))],
        out_specs=row(d), out_shape=jax.ShapeDtypeStruct((n, d), F32),
        compiler_params=_cparams(("parallel",)), name="merge",
    )(x, y_pool, y_dn, y_mem, gate_raw, w_branch, w_o)


NEG_INF = float("-inf")


def _top_values(x, k):
    vals = []
    for r in range(k):
        m = jnp.max(x, axis=0, keepdims=True)
        vals.append(m)
        if r + 1 < k:
            x = jnp.where(x == m, NEG_INF, x)
    return vals


def _peer_topk_kernel(q_ref, sub_ref, s2_ref, p2_ref, c1_ref, e1_ref):
    def head(hd, carry):
        s1 = _dot_nt(sub_ref[2 * hd], q_ref[2 * hd].astype(BF16))
        s2 = _dot_nt(sub_ref[2 * hd + 1], q_ref[2 * hd + 1].astype(BF16))
        t1 = _top_values(s1, PEER_TOPK)
        t2 = jnp.concatenate(_top_values(s2, PEER_TOPK), axis=0)
        cands = [a + t2 for a in t1]
        best = _top_values(jnp.concatenate(cands, axis=0), PEER_TOPK)
        m1, m2 = t1[0], t2[0:1]
        zsum = jnp.ones_like(m1)
        for r in range(1, PEER_TOPK):
            zsum = zsum + jnp.exp(best[r] - best[0])
        thr = best[PEER_TOPK - 1]
        c1 = jnp.full_like(s1, jnp.inf)
        for a, ca in zip(t1, cands):
            cut = jnp.min(jnp.where(ca >= thr, t2, jnp.inf), axis=0, keepdims=True)
            c1 = jnp.where(s1 == a, cut, c1)
        s2_ref[hd] = s2
        p2_ref[hd] = jnp.exp(s2 - m2)
        c1_ref[hd] = c1
        e1_ref[hd] = jnp.exp(s1 - m1) / zsum
        return carry

    lax.fori_loop(0, PEER_HEADS, head, 0)


def peer_topk(qs, subkeys, *, tm=256):
    nq, n, c = qs.shape
    tm = min(tm, n)
    o_spec = pl.BlockSpec((PEER_HEADS, PEER_NKEYS, tm), lambda i: (0, 0, i))
    o_shape = jax.ShapeDtypeStruct((PEER_HEADS, PEER_NKEYS, n), F32)
    return pl.pallas_call(
        _peer_topk_kernel, grid=(n // tm,),
        in_specs=[pl.BlockSpec((nq, tm, c), lambda i: (0, i, 0)),
                  pl.BlockSpec(subkeys.shape, lambda i: (0, 0, 0))],
        out_specs=[o_spec] * 4, out_shape=[o_shape] * 4,
        compiler_params=_cparams(("parallel",)), name="peer_topk",
    )(qs, subkeys)


PEER_ROWS = 8
PEER_TE = PEER_ROWS * PEER_NKEYS


def _gelu_tanh(x):
    return 0.5 * x * (1.0 + jnp.tanh(math.sqrt(2.0 / math.pi) * (x + 0.044715 * (x * x * x))))


def _peer_dense_kernel(ht_ref, u_ref, vt_ref, s2_ref, p2_ref, c1_ref, e1_ref, x_ref, o_ref,
                       acc_ref, a_ref, w_ref):
    e = pl.program_id(1)

    @pl.when(e == 0)
    def _():
        acc_ref[...] = jnp.zeros_like(acc_ref)

    a_ref[...] = _dot(u_ref[...], ht_ref[...])

    def rows(r, carry):
        gate = None
        for hd in range(PEER_HEADS):
            c1 = c1_ref[hd, pl.ds(r, 1), :]
            e1 = e1_ref[hd, pl.ds(r, 1), :]
            t = jnp.where(s2_ref[hd] >= c1, p2_ref[hd], 0.0) * e1
            gate = t if gate is None else gate + t
        r0 = pl.multiple_of(r * PEER_NKEYS, PEER_NKEYS)
        act = _gelu_tanh(a_ref[pl.ds(r0, PEER_NKEYS), :])
        w_ref[pl.ds(r0, PEER_NKEYS), :] = (gate * act).astype(BF16)
        return carry

    lax.fori_loop(0, PEER_ROWS, rows, 0)
    acc_ref[...] += _dot(vt_ref[...], w_ref[...])

    @pl.when(e == pl.num_programs(1) - 1)
    def _():
        o_ref[...] = x_ref[...] + acc_ref[...].T


def peer_dense(ht, u, vt, s2, p2, c1, e1, x, *, tm=512):
    n, d = x.shape
    tm = min(tm, n)
    tok = pl.BlockSpec((PEER_HEADS, PEER_NKEYS, tm), lambda i, e: (0, 0, i))
    key = pl.BlockSpec((PEER_HEADS, PEER_ROWS, tm), lambda i, e: (0, e, i))
    return pl.pallas_call(
        _peer_dense_kernel, grid=(n // tm, PEER_N // PEER_TE),
        in_specs=[pl.BlockSpec((d, tm), lambda i, e: (0, i)),
                  pl.BlockSpec((PEER_TE, d), lambda i, e: (e, 0)),
                  pl.BlockSpec((d, PEER_TE), lambda i, e: (0, e)),
                  tok, tok, key, key,
                  pl.BlockSpec((tm, d), lambda i, e: (i, 0))],
        out_specs=pl.BlockSpec((tm, d), lambda i, e: (i, 0)),
        out_shape=jax.ShapeDtypeStruct((n, d), F32),
        scratch_shapes=[pltpu.VMEM((d, tm), F32), pltpu.VMEM((PEER_TE, tm), F32),
                        pltpu.VMEM((PEER_TE, tm), BF16)],
        compiler_params=_cparams(("parallel", "arbitrary")), name="peer_dense",
    )(ht, u, vt, s2, p2, c1, e1, x)


def _final_norm_kernel(x_ref, g_ref, o_ref):
    o_ref[...] = _rms(x_ref[...], g_ref[...])


def final_norm(x, g, *, tm=512):
    n, d = x.shape
    tm = min(tm, n)
    return pl.pallas_call(
        _final_norm_kernel, grid=(n // tm,),
        in_specs=[pl.BlockSpec((tm, d), lambda i: (i, 0)), pl.BlockSpec((1, d), lambda i: (0, 0))],
        out_specs=pl.BlockSpec((tm, d), lambda i: (i, 0)),
        out_shape=jax.ShapeDtypeStruct((n, d), F32),
        compiler_params=_cparams(("parallel",)), name="final_norm",
    )(x, g.reshape(1, d))


def _layer_weights(l, g_mix, w_in, w_conv, a_log, dt_bias, g_dn_out, w_pool_grp, pool_scale, g_mem,
                   w_mem_kv, w_branch, w_o, g_ffn, w_peer_q, peer_subkeys, peer_u, peer_v):
    wi = w_in[l]
    ba_cols = jnp.pad(wi[:, OFF_BETA:OFF_XQ], ((0, 0), (0, LANES - 2 * DN_HEADS)))
    w_in_parts = [wi[:, :OFF_Q], wi[:, OFF_Q:OFF_Z], wi[:, OFF_Z:OFF_BETA], wi[:, OFF_XQ:OFF_GATE],
                  wi[:, OFF_GATE:], ba_cols]
    return dict(
        g_mix=g_mix[l], w_in=[w.astype(BF16) for w in w_in_parts], w_conv=w_conv[l], a_log=a_log[l],
        dt_bias=dt_bias[l], g_dn_out=g_dn_out[l], w_pool=w_pool_grp[l].astype(BF16),
        pool_scale=pool_scale[l], g_mem=g_mem[l],
        w_mem=[w_mem_kv[l][:, :BRANCH_W].astype(BF16), w_mem_kv[l][:, BRANCH_W:].astype(BF16)],
        w_branch=w_branch[l].astype(BF16), w_o=w_o[l].astype(BF16), g_ffn=g_ffn[l],
        w_peer_q=w_peer_q[l].astype(BF16),
        subkeys=peer_subkeys[l].reshape(2 * PEER_HEADS, PEER_NKEYS, PEER_HALF).astype(BF16),
        peer_u=peer_u[l].astype(BF16), peer_vt=peer_v[l].T.astype(BF16))


def _layer(x, start, mk, mv, pool_buf, conv_buf, s0, wl):
    b, t, d = x.shape
    n = b * t
    xf = x.reshape(n, d)
    u_pool, qkv, z, xq, gate_raw, ba = norm_matmul(xf, wl["g_mix"], wl["w_in"])
    u_pool = u_pool.reshape(b, t, BRANCH_W)
    qkv = qkv.reshape(b, t, 3 * BRANCH_W)
    y_pool = pool_mixer(u_pool, pool_buf, wl["w_pool"], wl["pool_scale"], start)
    q, k, v, bg = conv_prep(qkv, conv_buf, wl["w_conv"], ba.reshape(b, t, LANES), wl["a_log"],
                            wl["dt_bias"])
    c = min(DN_CHUNK, -(-t // SUBLANES) * SUBLANES)
    tp = -(-t // c) * c
    z3 = z.reshape(b, t, BRANCH_W)
    if tp != t:
        padt = lambda a: jnp.pad(a, ((0, 0), (0, tp - t), (0, 0)))
        q, k, v, bg, z3 = (padt(a) for a in (q, k, v, bg, z3))
    y_dn, s_new = gated_delta(q, k, v, bg, z3, s0, wl["g_dn_out"], c=c)
    y_dn = y_dn[:, :t]
    y_mem = mem_attend(xq.reshape(b, t, BRANCH_W), mk, mv)
    x2 = merge(xf, y_pool.reshape(n, BRANCH_W), y_dn.reshape(n, BRANCH_W),
               y_mem.reshape(n, BRANCH_W), gate_raw, wl["w_branch"], wl["w_o"])
    qs, hb = peer_query(x2, wl["g_ffn"], wl["w_peer_q"])
    s2, p2, c1, e1 = peer_topk(qs, wl["subkeys"])
    x3 = peer_dense(hb.T, wl["peer_u"], wl["peer_vt"], s2, p2, c1, e1, x2)
    new_pool = jnp.concatenate([pool_buf, u_pool], axis=1)[:, -POOL_BUF:]
    new_conv = jnp.concatenate([conv_buf, qkv], axis=1)[:, -(DN_CONV - 1):]
    return x3.reshape(b, t, d), new_pool, new_conv, s_new


def kernel(x_prompt, x_sample, state_pool, state_conv, state_delta, cache_mem_k, cache_mem_v, mem_prompt, g_mix, w_in, w_conv, a_log, dt_bias, g_dn_out, w_pool_grp, pool_scale, g_mem, w_mem_kv, w_branch, w_o, g_ffn, w_peer_q, peer_subkeys, peer_u, peer_v, g_final):
    bp = x_prompt.shape[0]
    bs = x_sample.shape[0]
    xp, xs = x_prompt, x_sample
    outs_p = [[] for _ in range(5)]
    outs_s = [[] for _ in range(3)]
    mem_flat = mem_prompt.reshape(bp * N_MEM, D_MODEL)
    for l in range(DEPTH):
        wl = _layer_weights(l, g_mix, w_in, w_conv, a_log, dt_bias, g_dn_out, w_pool_grp, pool_scale,
                            g_mem, w_mem_kv, w_branch, w_o, g_ffn, w_peer_q, peer_subkeys, peer_u,
                            peer_v)
        mk, mv = norm_matmul(mem_flat, wl["g_mem"], wl["w_mem"])
        mk = mk.reshape(bp, N_MEM, BRANCH_W)
        mv = mv.reshape(bp, N_MEM, BRANCH_W)
        xp, pool_p, conv_p, delta_p = _layer(
            xp, 0, mk, mv, jnp.zeros((bp, POOL_BUF, BRANCH_W), F32),
            jnp.zeros((bp, DN_CONV - 1, 3 * BRANCH_W), F32),
            jnp.zeros((bp, DN_HEADS, DN_HD, DN_HD), F32), wl)
        for lst, a in zip(outs_p, (pool_p, conv_p, delta_p,
                                   mk.reshape(bp, N_MEM, XA_HEADS, XA_HD),
                                   mv.reshape(bp, N_MEM, XA_HEADS, XA_HD))):
            lst.append(a)
        xs, pool_s, conv_s, delta_s = _layer(
            xs, PAST_LEN, cache_mem_k[l].reshape(bs, N_MEM, BRANCH_W),
            cache_mem_v[l].reshape(bs, N_MEM, BRANCH_W), state_pool[l], state_conv[l],
            state_delta[l], wl)
        for lst, a in zip(outs_s, (pool_s, conv_s, delta_s)):
            lst.append(a)
    y_prompt = final_norm(xp.reshape(-1, D_MODEL), g_final).reshape(xp.shape)
    y_sample = final_norm(xs.reshape(-1, D_MODEL), g_final).reshape(xs.shape)
    return (y_prompt, y_sample) + tuple(jnp.stack(o, axis=0) for o in outs_p) \
        + tuple(jnp.stack(o, axis=0) for o in outs_s)
```

```python
import functools
import math

import jax
import jax.numpy as jnp
from jax import lax
from jax.experimental import pallas as pl
from jax.experimental.pallas import tpu as pltpu

F32 = jnp.float32
BF16 = jnp.bfloat16
HIGHEST = lax.Precision.HIGHEST

D_MODEL = 1024
DEPTH = 2
PAST_LEN = 16384
N_MEM = 256
EPS = 1e-6
BRANCH_W = D_MODEL // 2
POOL_WINDOWS = (2, 4, 8, 16)
POOL_GW = BRANCH_W // len(POOL_WINDOWS)
POOL_BUF = max(POOL_WINDOWS) - 1
DN_HEADS = 4
DN_HD = BRANCH_W // DN_HEADS
DN_CONV = 4
DN_CHUNK = 64
XA_HEADS = 4
XA_HD = BRANCH_W // XA_HEADS
PEER_HEADS = 8
PEER_NKEYS = 128
PEER_N = PEER_NKEYS * PEER_NKEYS
PEER_HALF = 128
PEER_TOPK = 16

OFF_Q = BRANCH_W
OFF_Z = OFF_Q + 3 * BRANCH_W
OFF_BETA = OFF_Z + BRANCH_W
OFF_ALPHA = OFF_BETA + DN_HEADS
OFF_XQ = OFF_ALPHA + DN_HEADS
OFF_GATE = OFF_XQ + BRANCH_W

LANES = 128
SUBLANES = 8
BF16_ROWS = 16
POOL_HIST = 16
CONV_HIST = SUBLANES
VMEM_LIMIT = 56 * 1024 * 1024


def _cparams(sem):
    return pltpu.CompilerParams(dimension_semantics=sem, vmem_limit_bytes=VMEM_LIMIT)


def _sigmoid(x):
    return 1.0 / (1.0 + jnp.exp(-x))


def _dot(a, b, precision=None):
    return jnp.dot(a, b, preferred_element_type=F32, precision=precision)


def _dot_nt(a, b, precision=None):
    return lax.dot_general(a, b, (((1,), (1,)), ((), ())), preferred_element_type=F32,
                           precision=precision)


def _split(x):
    hi = x.astype(BF16)
    return hi, (x - hi.astype(F32)).astype(BF16)


def _split_matmul(lhs_list, rhs):
    n, rows = len(lhs_list), lhs_list[0].shape[0]
    his, los = zip(*[_split(l) for l in lhs_list])
    rhs_hi, rhs_lo = _split(rhs)
    r1 = _dot(jnp.concatenate(list(his) + list(los), axis=0), rhs_hi)
    r2 = _dot(jnp.concatenate(list(his), axis=0), rhs_lo) if n > 1 else _dot(his[0], rhs_lo)
    blk = lambda r, i: r[i * rows:(i + 1) * rows]
    return [blk(r1, i) + blk(r1, n + i) + blk(r2, i) for i in range(n)]


def _dot_tn(a, b, precision=None):
    return lax.dot_general(a, b, (((0,), (0,)), ((), ())), preferred_element_type=F32,
                           precision=precision)


def _rms(x, g):
    return x * lax.rsqrt(jnp.mean(x * x, axis=-1, keepdims=True) + EPS) * g


def _norm_matmul_kernel(x_ref, g_ref, *refs, n_w, col_chunk):
    w_refs = refs[:n_w]
    o_refs = refs[n_w:2 * n_w]
    hb = _rms(x_ref[...], g_ref[...]).astype(BF16)
    for w_ref, o_ref in zip(w_refs, o_refs):
        cols = w_ref.shape[1]
        for c0 in range(0, cols, col_chunk):
            c1 = min(cols, c0 + col_chunk)
            o_ref[:, c0:c1] = _dot(hb, w_ref[:, c0:c1])


def norm_matmul(x, g, ws, *, tm=256):
    n, d = x.shape
    tm = min(tm, n)
    in_specs = [pl.BlockSpec((tm, d), lambda i: (i, 0)), pl.BlockSpec((1, d), lambda i: (0, 0))]
    in_specs += [pl.BlockSpec(w.shape, lambda i: (0, 0)) for w in ws]
    out_specs = [pl.BlockSpec((tm, w.shape[1]), lambda i: (i, 0)) for w in ws]
    out_shape = [jax.ShapeDtypeStruct((n, w.shape[1]), F32) for w in ws]
    return pl.pallas_call(
        functools.partial(_norm_matmul_kernel, n_w=len(ws), col_chunk=512),
        grid=(n // tm,), in_specs=in_specs, out_specs=out_specs, out_shape=out_shape,
        compiler_params=_cparams(("parallel",)), name="norm_matmul",
    )(x, g.reshape(1, d), *ws)


def _peer_query_kernel(x_ref, g_ref, w_ref, q_ref, h_ref):
    hb = _rms(x_ref[...], g_ref[...]).astype(BF16)
    h_ref[...] = hb
    for hd in range(PEER_HEADS):
        q = _dot(hb, w_ref[:, hd * 2 * PEER_HALF:(hd + 1) * 2 * PEER_HALF])
        q_ref[2 * hd] = q[:, :PEER_HALF]
        q_ref[2 * hd + 1] = q[:, PEER_HALF:]


def peer_query(x, g, w, *, tm=256):
    n, d = x.shape
    tm = min(tm, n)
    nq = 2 * PEER_HEADS
    return pl.pallas_call(
        _peer_query_kernel, grid=(n // tm,),
        in_specs=[pl.BlockSpec((tm, d), lambda i: (i, 0)), pl.BlockSpec((1, d), lambda i: (0, 0)),
                  pl.BlockSpec(w.shape, lambda i: (0, 0))],
        out_specs=[pl.BlockSpec((nq, tm, PEER_HALF), lambda i: (0, i, 0)),
                   pl.BlockSpec((tm, d), lambda i: (i, 0))],
        out_shape=[jax.ShapeDtypeStruct((nq, n, PEER_HALF), F32), jax.ShapeDtypeStruct((n, d), BF16)],
        compiler_params=_cparams(("parallel",)), name="peer_query",
    )(x, g.reshape(1, d), w)


def _pool_kernel(u_ref, buf_ref, w_ref, scale_ref, y_ref, ext_ref, *, tt, start):
    ti = pl.program_id(1)

    @pl.when(ti == 0)
    def _():
        ext_ref[0:POOL_HIST, :] = buf_ref[0]

    ext_ref[POOL_HIST:POOL_HIST + tt, :] = u_ref[0]
    pos = start + ti * tt + lax.broadcasted_iota(jnp.int32, (tt, 1), 0)
    for gi, win in enumerate(POOL_WINDOWS):
        sl = slice(gi * POOL_GW, (gi + 1) * POOL_GW)
        u = ext_ref[POOL_HIST:POOL_HIST + tt, sl]
        s = u
        for k in range(1, win):
            s = s + ext_ref[POOL_HIST - k:POOL_HIST - k + tt, sl]
        cnt = jnp.minimum(pos + 1, win).astype(F32)
        d = s / cnt - u
        y_ref[0, :, sl] = _dot(d.astype(BF16), w_ref[gi]) * scale_ref[:, sl]
    ext_ref[0:POOL_HIST, :] = ext_ref[tt:tt + POOL_HIST, :]


def pool_mixer(u, buf, w_grp, scale, start, *, tt=512):
    b, t, c = u.shape
    tt = min(tt, t)
    hist = jnp.pad(buf, ((0, 0), (POOL_HIST - POOL_BUF, 0), (0, 0)))
    return pl.pallas_call(
        functools.partial(_pool_kernel, tt=tt, start=start), grid=(b, t // tt),
        in_specs=[pl.BlockSpec((1, tt, c), lambda i, j: (i, j, 0)),
                  pl.BlockSpec((1, POOL_HIST, c), lambda i, j: (i, 0, 0)),
                  pl.BlockSpec(w_grp.shape, lambda i, j: (0, 0, 0)),
                  pl.BlockSpec((1, c), lambda i, j: (0, 0))],
        out_specs=pl.BlockSpec((1, tt, c), lambda i, j: (i, j, 0)),
        out_shape=jax.ShapeDtypeStruct((b, t, c), F32),
        scratch_shapes=[pltpu.VMEM((POOL_HIST + tt, c), F32)],
        compiler_params=_cparams(("parallel", "arbitrary")), name="pool_mixer",
    )(u, hist, w_grp, scale.reshape(1, c))


def _softplus(x):
    return jnp.maximum(x, 0.0) + jnp.log(1.0 + jnp.exp(-jnp.abs(x)))


def _conv_kernel(x_ref, buf_ref, w_ref, ba_ref, alog_ref, dtb_ref, q_ref, k_ref, v_ref, bg_ref,
                 ext_ref, *, tt):
    ti = pl.program_id(1)

    @pl.when(ti == 0)
    def _():
        ext_ref[0:CONV_HIST, :] = buf_ref[0]

    ext_ref[CONV_HIST:CONV_HIST + tt, :] = x_ref[0]
    base = CONV_HIST - (DN_CONV - 1)
    y = ext_ref[base:base + tt, :] * w_ref[0:1, :]
    for j in range(1, DN_CONV):
        y = y + ext_ref[base + j:base + j + tt, :] * w_ref[j:j + 1, :]
    y = y * _sigmoid(y)
    for part, o_ref in enumerate((q_ref, k_ref, v_ref)):
        for h in range(DN_HEADS):
            a = y[:, part * BRANCH_W + h * DN_HD:part * BRANCH_W + (h + 1) * DN_HD]
            if part < 2:
                a = a * lax.rsqrt(jnp.sum(a * a, axis=-1, keepdims=True) + EPS)
            o_ref[0, :, h * DN_HD:(h + 1) * DN_HD] = a
    ba = ba_ref[0]
    beta = _sigmoid(ba)
    g = -jnp.exp(alog_ref[...]) * _softplus(ba + dtb_ref[...])
    lane = lax.broadcasted_iota(jnp.int32, ba.shape, 1)
    bg_ref[0] = jnp.where(lane < DN_HEADS, beta, g)
    ext_ref[0:CONV_HIST, :] = ext_ref[tt:tt + CONV_HIST, :]


def conv_prep(qkv, buf, w_conv, ba, a_log, dt_bias, *, tt=256):
    b, t, c = qkv.shape
    tt = min(tt, t)
    hist = jnp.pad(buf, ((0, 0), (CONV_HIST - (DN_CONV - 1), 0), (0, 0)))
    pad = jnp.zeros((LANES - 2 * DN_HEADS,), F32)
    alog_v = jnp.concatenate([jnp.zeros((DN_HEADS,), F32), a_log, pad]).reshape(1, LANES)
    dtb_v = jnp.concatenate([jnp.zeros((DN_HEADS,), F32), dt_bias, pad]).reshape(1, LANES)
    seq = lambda w: pl.BlockSpec((1, tt, w), lambda i, j: (i, j, 0))
    return pl.pallas_call(
        functools.partial(_conv_kernel, tt=tt), grid=(b, t // tt),
        in_specs=[seq(c), pl.BlockSpec((1, CONV_HIST, c), lambda i, j: (i, 0, 0)),
                  pl.BlockSpec(w_conv.shape, lambda i, j: (0, 0)), seq(LANES),
                  pl.BlockSpec((1, LANES), lambda i, j: (0, 0)),
                  pl.BlockSpec((1, LANES), lambda i, j: (0, 0))],
        out_specs=[seq(BRANCH_W), seq(BRANCH_W), seq(BRANCH_W), seq(LANES)],
        out_shape=[jax.ShapeDtypeStruct((b, t, BRANCH_W), F32)] * 3
        + [jax.ShapeDtypeStruct((b, t, LANES), F32)],
        scratch_shapes=[pltpu.VMEM((CONV_HIST + tt, c), F32)],
        compiler_params=_cparams(("parallel", "arbitrary")), name="conv_prep",
    )(qkv, hist, w_conv, ba, alog_v, dtb_v)


def _delta_kernel(q_ref, k_ref, v_ref, bg_ref, bgt_ref, z_ref, s0_ref, gdn_ref, y_ref, sfin_ref,
                  s_ref, *, c):
    n = pl.program_id(1)

    @pl.when(n == 0)
    def _():
        s_ref[...] = s0_ref[0]

    row = lax.broadcasted_iota(jnp.int32, (c, c), 0)
    col = lax.broadcasted_iota(jnp.int32, (c, c), 1)
    ltri = (row >= col).astype(F32)
    bg = bg_ref[0]
    gc_all = _dot(ltri, bg, HIGHEST)
    gct_all = _dot_nt(bgt_ref[0, 0], ltri, HIGHEST)
    n_double = int(round(math.log2(c))) - 1
    shift = int(round(math.log2(c)))
    heads = range(DN_HEADS)
    hc = DN_HEADS * c

    def stack(x):
        return jnp.concatenate([x[:, h * DN_HD:(h + 1) * DN_HD] for h in heads], axis=0)

    def rows(x, h):
        return x[h * c:(h + 1) * c]

    brow = lax.broadcasted_iota(jnp.int32, (hc, hc), 0)
    bcol = lax.broadcasted_iota(jnp.int32, (hc, hc), 1)
    same = lax.shift_right_logical(brow, shift) == lax.shift_right_logical(bcol, shift)
    tril = same & (brow >= bcol)
    strict = same & (brow > bcol)
    eye = (brow == bcol).astype(F32)
    q = stack(q_ref[0]) * (DN_HD ** -0.5)
    k = stack(k_ref[0])
    v = stack(v_ref[0])
    beta = jnp.concatenate([bg[:, h:h + 1] for h in heads], axis=0)
    gc = jnp.concatenate([gc_all[:, DN_HEADS + h:DN_HEADS + h + 1] for h in heads], axis=0)
    gcr = jnp.concatenate([gct_all[DN_HEADS + h:DN_HEADS + h + 1, :] for h in heads], axis=1)
    gls = [gc_all[c - 1:c, DN_HEADS + h:DN_HEADS + h + 1] for h in heads]
    gl = jnp.concatenate([jnp.broadcast_to(g, (c, 1)) for g in gls], axis=0)
    decay = jnp.where(tril, jnp.exp(jnp.where(tril, gc - gcr, 0.0)), 0.0)
    kb = k * beta
    vb = v * beta
    k16 = k.astype(BF16)
    kq = _dot_nt(jnp.concatenate([kb.astype(BF16), q.astype(BF16)], axis=0), k16)
    a = jnp.where(strict, kq[:hc] * decay, 0.0)
    attn = jnp.where(tril, kq[hc:] * decay, 0.0)
    x = -a
    tinv = eye + x
    (p,) = _split_matmul([x], x)
    for m in range(n_double):
        if m + 1 < n_double:
            tp, p = _split_matmul([tinv, p], p)
        else:
            (tp,) = _split_matmul([tinv], p)
        tinv = tinv + tp
    egc = jnp.exp(gc)
    (uw,) = _split_matmul([tinv], jnp.concatenate([vb, kb * egc], axis=1))
    u = uw[:, :DN_HD]
    w16 = uw[:, DN_HD:].astype(BF16)
    qg16 = (q * egc).astype(BF16)
    kd16 = (k * jnp.exp(gl - gc)).astype(BF16)
    states = [s_ref[h] for h in heads]
    ws = [_dot(jnp.concatenate([rows(w16, h), rows(qg16, h)], axis=0), states[h].astype(BF16))
          for h in heads]
    v_new = u - jnp.concatenate([r[:c] for r in ws], axis=0)
    v16 = v_new.astype(BF16)
    o = jnp.concatenate([r[c:] for r in ws], axis=0) + _dot(attn.astype(BF16), v16)
    for h in heads:
        s_ref[h] = states[h] * jnp.exp(gls[h]) + _dot_tn(rows(kd16, h), rows(v16, h))
    o = o * lax.rsqrt(jnp.mean(o * o, axis=-1, keepdims=True) + EPS) * gdn_ref[...]
    z = stack(z_ref[0])
    o = o * (z * _sigmoid(z))
    for h in heads:
        y_ref[0, :, h * DN_HD:(h + 1) * DN_HD] = rows(o, h)

    @pl.when(n == pl.num_programs(1) - 1)
    def _():
        sfin_ref[0] = s_ref[...]


def gated_delta(q, k, v, bg, z, s0, g_dn_out, *, c):
    b, t, _ = q.shape
    nc = t // c
    bgt = jnp.swapaxes(bg[..., :SUBLANES].reshape(b, nc, c, SUBLANES), 2, 3)
    seq = lambda w: pl.BlockSpec((1, c, w), lambda i, j: (i, j, 0))
    st = pl.BlockSpec((1, DN_HEADS, DN_HD, DN_HD), lambda i, j: (i, 0, 0, 0))
    return pl.pallas_call(
        functools.partial(_delta_kernel, c=c), grid=(b, nc),
        in_specs=[seq(BRANCH_W), seq(BRANCH_W), seq(BRANCH_W), seq(LANES),
                  pl.BlockSpec((1, 1, SUBLANES, c), lambda i, j: (i, j, 0, 0)), seq(BRANCH_W), st,
                  pl.BlockSpec((1, DN_HD), lambda i, j: (0, 0))],
        out_specs=[seq(BRANCH_W), st],
        out_shape=[jax.ShapeDtypeStruct((b, t, BRANCH_W), F32),
                   jax.ShapeDtypeStruct((b, DN_HEADS, DN_HD, DN_HD), F32)],
        scratch_shapes=[pltpu.VMEM((DN_HEADS, DN_HD, DN_HD), F32)],
        compiler_params=_cparams(("parallel", "arbitrary")), name="gated_delta",
    )(q, k, v, bg, bgt, z, s0, g_dn_out.reshape(1, DN_HD))


def _attn_kernel(q_ref, k_ref, v_ref, o_ref):
    for h in range(XA_HEADS):
        sl = slice(h * XA_HD, (h + 1) * XA_HD)
        q = q_ref[0, :, sl].astype(BF16)
        k = k_ref[0, :, sl].astype(BF16)
        v = v_ref[0, :, sl].astype(BF16)
        s = _dot_nt(q, k) * (XA_HD ** -0.5)
        p = jnp.exp(s - jnp.max(s, axis=-1, keepdims=True))
        p = p / jnp.sum(p, axis=-1, keepdims=True)
        o_ref[0, :, sl] = _dot(p.astype(BF16), v)


def mem_attend(xq, mk, mv, *, tq=512):
    b, t, c = xq.shape
    tq = min(tq, t)
    kv = pl.BlockSpec((1, N_MEM, c), lambda i, j: (i, 0, 0))
    return pl.pallas_call(
        _attn_kernel, grid=(b, t // tq),
        in_specs=[pl.BlockSpec((1, tq, c), lambda i, j: (i, j, 0)), kv, kv],
        out_specs=pl.BlockSpec((1, tq, c), lambda i, j: (i, j, 0)),
        out_shape=jax.ShapeDtypeStruct((b, t, c), F32),
        compiler_params=_cparams(("parallel", "parallel")), name="mem_attend",
    )(xq, mk, mv)


def _merge_kernel(x_ref, yp_ref, yd_ref, ym_ref, gate_ref, wb_ref, wo_ref, o_ref):
    acc = None
    for n, y_ref in enumerate((yp_ref, yd_ref, ym_ref)):
        bp = _dot(y_ref[...].astype(BF16), wb_ref[n])
        t = _sigmoid(gate_ref[:, n * D_MODEL:(n + 1) * D_MODEL]) * bp
        acc = t if acc is None else acc + t
    o_ref[...] = x_ref[...] + _dot(acc.astype(BF16), wo_ref[...])


def merge(x, y_pool, y_dn, y_mem, gate_raw, w_branch, w_o, *, tm=256):
    n, d = x.shape
    tm = min(tm, n)
    row = lambda w: pl.BlockSpec((tm, w), lambda i: (i, 0))
    return pl.pallas_call(
        _merge_kernel, grid=(n // tm,),
        in_specs=[row(d), row(BRANCH_W), row(BRANCH_W), row(BRANCH_W), row(3 * d),
                  pl.BlockSpec(w_branch.shape, lambda i: (0, 0, 0)),
                  pl.BlockSpec(w_o.shape, lambda i: (0, 0))],
        out_specs=row(d), out_shape=jax.ShapeDtypeStruct((n, d), F32),
        compiler_params=_cparams(("parallel",)), name="merge",
    )(x, y_pool, y_dn, y_mem, gate_raw, w_branch, w_o)


NEG_INF = float("-inf")


def _top_values(x, k):
    vals = []
    for r in range(k):
        m = jnp.max(x, axis=0, keepdims=True)
        vals.append(m)
        if r + 1 < k:
            x = jnp.where(x == m, NEG_INF, x)
    return vals


def _peer_topk_kernel(q_ref, sub_ref, s2_ref, p2_ref, c1_ref, e1_ref):
    def head(hd, carry):
        s1 = _dot_nt(sub_ref[2 * hd], q_ref[2 * hd].astype(BF16))
        s2 = _dot_nt(sub_ref[2 * hd + 1], q_ref[2 * hd + 1].astype(BF16))
        t1 = _top_values(s1, PEER_TOPK)
        t2 = jnp.concatenate(_top_values(s2, PEER_TOPK), axis=0)
        cands = [a + t2 for a in t1]
        best = _top_values(jnp.concatenate(cands, axis=0), PEER_TOPK)
        m1, m2 = t1[0], t2[0:1]
        zsum = jnp.ones_like(m1)
        for r in range(1, PEER_TOPK):
            zsum = zsum + jnp.exp(best[r] - best[0])
        thr = best[PEER_TOPK - 1]
        c1 = jnp.full_like(s1, jnp.inf)
        for a, ca in zip(t1, cands):
            cut = jnp.min(jnp.where(ca >= thr, t2, jnp.inf), axis=0, keepdims=True)
            c1 = jnp.where(s1 == a, cut, c1)
        s2_ref[hd] = s2
        p2_ref[hd] = jnp.exp(s2 - m2)
        c1_ref[hd] = c1
        e1_ref[hd] = jnp.exp(s1 - m1) / zsum
        return carry

    lax.fori_loop(0, PEER_HEADS, head, 0)


def peer_topk(qs, subkeys, *, tm=256):
    nq, n, c = qs.shape
    tm = min(tm, n)
    o_spec = pl.BlockSpec((PEER_HEADS, PEER_NKEYS, tm), lambda i: (0, 0, i))
    o_shape = jax.ShapeDtypeStruct((PEER_HEADS, PEER_NKEYS, n), F32)
    return pl.pallas_call(
        _peer_topk_kernel, grid=(n // tm,),
        in_specs=[pl.BlockSpec((nq, tm, c), lambda i: (0, i, 0)),
                  pl.BlockSpec(subkeys.shape, lambda i: (0, 0, 0))],
        out_specs=[o_spec] * 4, out_shape=[o_shape] * 4,
        compiler_params=_cparams(("parallel",)), name="peer_topk",
    )(qs, subkeys)


PEER_ROWS = 8
PEER_TE = PEER_ROWS * PEER_NKEYS


def _gelu_tanh(x):
    return 0.5 * x * (1.0 + jnp.tanh(math.sqrt(2.0 / math.pi) * (x + 0.044715 * (x * x * x))))


def _peer_dense_kernel(ht_ref, u_ref, vt_ref, s2_ref, p2_ref, c1_ref, e1_ref, x_ref, o_ref,
                       acc_ref, a_ref, w_ref):
    e = pl.program_id(1)

    @pl.when(e == 0)
    def _():
        acc_ref[...] = jnp.zeros_like(acc_ref)

    a_ref[...] = _dot(u_ref[...], ht_ref[...])

    def rows(r, carry):
        gate = None
        for hd in range(PEER_HEADS):
            c1 = c1_ref[hd, pl.ds(r, 1), :]
            e1 = e1_ref[hd, pl.ds(r, 1), :]
            t = jnp.where(s2_ref[hd] >= c1, p2_ref[hd], 0.0) * e1
            gate = t if gate is None else gate + t
        r0 = pl.multiple_of(r * PEER_NKEYS, PEER_NKEYS)
        act = _gelu_tanh(a_ref[pl.ds(r0, PEER_NKEYS), :])
        w_ref[pl.ds(r0, PEER_NKEYS), :] = (gate * act).astype(BF16)
        return carry

    lax.fori_loop(0, PEER_ROWS, rows, 0)
    acc_ref[...] += _dot(vt_ref[...], w_ref[...])

    @pl.when(e == pl.num_programs(1) - 1)
    def _():
        o_ref[...] = x_ref[...] + acc_ref[...].T


def peer_dense(ht, u, vt, s2, p2, c1, e1, x, *, tm=512):
    n, d = x.shape
    tm = min(tm, n)
    tok = pl.BlockSpec((PEER_HEADS, PEER_NKEYS, tm), lambda i, e: (0, 0, i))
    key = pl.BlockSpec((PEER_HEADS, PEER_ROWS, tm), lambda i, e: (0, e, i))
    return pl.pallas_call(
        _peer_dense_kernel, grid=(n // tm, PEER_N // PEER_TE),
        in_specs=[pl.BlockSpec((d, tm), lambda i, e: (0, i)),
                  pl.BlockSpec((PEER_TE, d), lambda i, e: (e, 0)),
                  pl.BlockSpec((d, PEER_TE), lambda i, e: (0, e)),
                  tok, tok, key, key,
                  pl.BlockSpec((tm, d), lambda i, e: (i, 0))],
        out_specs=pl.BlockSpec((tm, d), lambda i, e: (i, 0)),
        out_shape=jax.ShapeDtypeStruct((n, d), F32),
        scratch_shapes=[pltpu.VMEM((d, tm), F32), pltpu.VMEM((PEER_TE, tm), F32),
                        pltpu.VMEM((PEER_TE, tm), BF16)],
        compiler_params=_cparams(("parallel", "arbitrary")), name="peer_dense",
    )(ht, u, vt, s2, p2, c1, e1, x)


def _final_norm_kernel(x_ref, g_ref, o_ref):
    o_ref[...] = _rms(x_ref[...], g_ref[...])


def final_norm(x, g, *, tm=512):
    n, d = x.shape
    tm = min(tm, n)
    return pl.pallas_call(
        _final_norm_kernel, grid=(n // tm,),
        in_specs=[pl.BlockSpec((tm, d), lambda i: (i, 0)), pl.BlockSpec((1, d), lambda i: (0, 0))],
        out_specs=pl.BlockSpec((tm, d), lambda i: (i, 0)),
        out_shape=jax.ShapeDtypeStruct((n, d), F32),
        compiler_params=_cparams(("parallel",)), name="final_norm",
    )(x, g.reshape(1, d))


def _layer_weights(l, g_mix, w_in, w_conv, a_log, dt_bias, g_dn_out, w_pool_grp, pool_scale, g_mem,
                   w_mem_kv, w_branch, w_o, g_ffn, w_peer_q, peer_subkeys, peer_u, peer_v):
    wi = w_in[l]
    ba_cols = jnp.pad(wi[:, OFF_BETA:OFF_XQ], ((0, 0), (0, LANES - 2 * DN_HEADS)))
    w_in_parts = [wi[:, :OFF_Q], wi[:, OFF_Q:OFF_Z], wi[:, OFF_Z:OFF_BETA], wi[:, OFF_XQ:OFF_GATE],
                  wi[:, OFF_GATE:], ba_cols]
    return dict(
        g_mix=g_mix[l], w_in=[w.astype(BF16) for w in w_in_parts], w_conv=w_conv[l], a_log=a_log[l],
        dt_bias=dt_bias[l], g_dn_out=g_dn_out[l], w_pool=w_pool_grp[l].astype(BF16),
        pool_scale=pool_scale[l], g_mem=g_mem[l],
        w_mem=[w_mem_kv[l][:, :BRANCH_W].astype(BF16), w_mem_kv[l][:, BRANCH_W:].astype(BF16)],
        w_branch=w_branch[l].astype(BF16), w_o=w_o[l].astype(BF16), g_ffn=g_ffn[l],
        w_peer_q=w_peer_q[l].astype(BF16),
        subkeys=peer_subkeys[l].reshape(2 * PEER_HEADS, PEER_NKEYS, PEER_HALF).astype(BF16),
        peer_u=peer_u[l].astype(BF16), peer_vt=peer_v[l].T.astype(BF16))


def _layer(x, start, mk, mv, pool_buf, conv_buf, s0, wl):
    b, t, d = x.shape
    n = b * t
    xf = x.reshape(n, d)
    u_pool, qkv, z, xq, gate_raw, ba = norm_matmul(xf, wl["g_mix"], wl["w_in"])
    u_pool = u_pool.reshape(b, t, BRANCH_W)
    qkv = qkv.reshape(b, t, 3 * BRANCH_W)
    y_pool = pool_mixer(u_pool, pool_buf, wl["w_pool"], wl["pool_scale"], start)
    q, k, v, bg = conv_prep(qkv, conv_buf, wl["w_conv"], ba.reshape(b, t, LANES), wl["a_log"],
                            wl["dt_bias"])
    c = min(DN_CHUNK, -(-t // BF16_ROWS) * BF16_ROWS)
    tp = -(-t // c) * c
    z3 = z.reshape(b, t, BRANCH_W)
    if tp != t:
        padt = lambda a: jnp.pad(a, ((0, 0), (0, tp - t), (0, 0)))
        q, k, v, bg, z3 = (padt(a) for a in (q, k, v, bg, z3))
    y_dn, s_new = gated_delta(q, k, v, bg, z3, s0, wl["g_dn_out"], c=c)
    y_dn = y_dn[:, :t]
    y_mem = mem_attend(xq.reshape(b, t, BRANCH_W), mk, mv)
    x2 = merge(xf, y_pool.reshape(n, BRANCH_W), y_dn.reshape(n, BRANCH_W),
               y_mem.reshape(n, BRANCH_W), gate_raw, wl["w_branch"], wl["w_o"])
    qs, hb = peer_query(x2, wl["g_ffn"], wl["w_peer_q"])
    s2, p2, c1, e1 = peer_topk(qs, wl["subkeys"])
    x3 = peer_dense(hb.T, wl["peer_u"], wl["peer_vt"], s2, p2, c1, e1, x2)
    new_pool = jnp.concatenate([pool_buf, u_pool], axis=1)[:, -POOL_BUF:]
    new_conv = jnp.concatenate([conv_buf, qkv], axis=1)[:, -(DN_CONV - 1):]
    return x3.reshape(b, t, d), new_pool, new_conv, s_new


def kernel(x_prompt, x_sample, state_pool, state_conv, state_delta, cache_mem_k, cache_mem_v, mem_prompt, g_mix, w_in, w_conv, a_log, dt_bias, g_dn_out, w_pool_grp, pool_scale, g_mem, w_mem_kv, w_branch, w_o, g_ffn, w_peer_q, peer_subkeys, peer_u, peer_v, g_final):
    bp = x_prompt.shape[0]
    bs = x_sample.shape[0]
    xp, xs = x_prompt, x_sample
    outs_p = [[] for _ in range(5)]
    outs_s = [[] for _ in range(3)]
    mem_flat = mem_prompt.reshape(bp * N_MEM, D_MODEL)
    for l in range(DEPTH):
        wl = _layer_weights(l, g_mix, w_in, w_conv, a_log, dt_bias, g_dn_out, w_pool_grp, pool_scale,
                            g_mem, w_mem_kv, w_branch, w_o, g_ffn, w_peer_q, peer_subkeys, peer_u,
                            peer_v)
        mk, mv = norm_matmul(mem_flat, wl["g_mem"], wl["w_mem"])
        mk = mk.reshape(bp, N_MEM, BRANCH_W)
        mv = mv.reshape(bp, N_MEM, BRANCH_W)
        xp, pool_p, conv_p, delta_p = _layer(
            xp, 0, mk, mv, jnp.zeros((bp, POOL_BUF, BRANCH_W), F32),
            jnp.zeros((bp, DN_CONV - 1, 3 * BRANCH_W), F32),
            jnp.zeros((bp, DN_HEADS, DN_HD, DN_HD), F32), wl)
        for lst, a in zip(outs_p, (pool_p, conv_p, delta_p,
                                   mk.reshape(bp, N_MEM, XA_HEADS, XA_HD),
                                   mv.reshape(bp, N_MEM, XA_HEADS, XA_HD))):
            lst.append(a)
        xs, pool_s, conv_s, delta_s = _layer(
            xs, PAST_LEN, cache_mem_k[l].reshape(bs, N_MEM, BRANCH_W),
            cache_mem_v[l].reshape(bs, N_MEM, BRANCH_W), state_pool[l], state_conv[l],
            state_delta[l], wl)
        for lst, a in zip(outs_s, (pool_s, conv_s, delta_s)):
            lst.append(a)
    y_prompt = final_norm(xp.reshape(-1, D_MODEL), g_final).reshape(xp.shape)
    y_sample = final_norm(xs.reshape(-1, D_MODEL), g_final).reshape(xs.shape)
    return (y_prompt, y_sample) + tuple(jnp.stack(o, axis=0) for o in outs_p) \
        + tuple(jnp.stack(o, axis=0) for o in outs_s)
```

```python
import functools
import math

import jax
import jax.numpy as jnp
from jax import lax
from jax.experimental import pallas as pl
from jax.experimental.pallas import tpu as pltpu

F32 = jnp.float32
BF16 = jnp.bfloat16
HIGHEST = lax.Precision.HIGHEST

D_MODEL = 1024
DEPTH = 2
PAST_LEN = 16384
N_MEM = 256
EPS = 1e-6
BRANCH_W = D_MODEL // 2
POOL_WINDOWS = (2, 4, 8, 16)
POOL_GW = BRANCH_W // len(POOL_WINDOWS)
POOL_BUF = max(POOL_WINDOWS) - 1
DN_HEADS = 4
DN_HD = BRANCH_W // DN_HEADS
DN_CONV = 4
DN_CHUNK = 64
XA_HEADS = 4
XA_HD = BRANCH_W // XA_HEADS
PEER_HEADS = 8
PEER_NKEYS = 128
PEER_N = PEER_NKEYS * PEER_NKEYS
PEER_HALF = 128
PEER_TOPK = 16

OFF_Q = BRANCH_W
OFF_Z = OFF_Q + 3 * BRANCH_W
OFF_BETA = OFF_Z + BRANCH_W
OFF_ALPHA = OFF_BETA + DN_HEADS
OFF_XQ = OFF_ALPHA + DN_HEADS
OFF_GATE = OFF_XQ + BRANCH_W

LANES = 128
SUBLANES = 8
BF16_ROWS = 16
POOL_HIST = 16
CONV_HIST = SUBLANES
VMEM_LIMIT = 56 * 1024 * 1024


def _cparams(sem):
    return pltpu.CompilerParams(dimension_semantics=sem, vmem_limit_bytes=VMEM_LIMIT)


def _sigmoid(x):
    return 1.0 / (1.0 + jnp.exp(-x))


def _dot(a, b, precision=None):
    return jnp.dot(a, b, preferred_element_type=F32, precision=precision)


def _dot_nt(a, b, precision=None):
    return lax.dot_general(a, b, (((1,), (1,)), ((), ())), preferred_element_type=F32,
                           precision=precision)


def _split(x):
    hi = x.astype(BF16)
    return hi, (x - hi.astype(F32)).astype(BF16)


def _split_matmul(lhs_list, rhs):
    n, rows = len(lhs_list), lhs_list[0].shape[0]
    his, los = zip(*[_split(l) for l in lhs_list])
    rhs_hi, rhs_lo = _split(rhs)
    r1 = _dot(jnp.concatenate(list(his) + list(los), axis=0), rhs_hi)
    r2 = _dot(jnp.concatenate(list(his), axis=0), rhs_lo) if n > 1 else _dot(his[0], rhs_lo)
    blk = lambda r, i: r[i * rows:(i + 1) * rows]
    return [blk(r1, i) + blk(r1, n + i) + blk(r2, i) for i in range(n)]


def _dot_tn(a, b, precision=None):
    return lax.dot_general(a, b, (((0,), (0,)), ((), ())), preferred_element_type=F32,
                           precision=precision)


def _rms(x, g):
    return x * lax.rsqrt(jnp.mean(x * x, axis=-1, keepdims=True) + EPS) * g


def _norm_matmul_kernel(x_ref, g_ref, *refs, n_w, col_chunk):
    w_refs = refs[:n_w]
    o_refs = refs[n_w:2 * n_w]
    hb = _rms(x_ref[...], g_ref[...]).astype(BF16)
    for w_ref, o_ref in zip(w_refs, o_refs):
        cols = w_ref.shape[1]
        for c0 in range(0, cols, col_chunk):
            c1 = min(cols, c0 + col_chunk)
            o_ref[:, c0:c1] = _dot(hb, w_ref[:, c0:c1])


def norm_matmul(x, g, ws, *, tm=256):
    n, d = x.shape
    tm = min(tm, n)
    in_specs = [pl.BlockSpec((tm, d), lambda i: (i, 0)), pl.BlockSpec((1, d), lambda i: (0, 0))]
    in_specs += [pl.BlockSpec(w.shape, lambda i: (0, 0)) for w in ws]
    out_specs = [pl.BlockSpec((tm, w.shape[1]), lambda i: (i, 0)) for w in ws]
    out_shape = [jax.ShapeDtypeStruct((n, w.shape[1]), F32) for w in ws]
    return pl.pallas_call(
        functools.partial(_norm_matmul_kernel, n_w=len(ws), col_chunk=512),
        grid=(n // tm,), in_specs=in_specs, out_specs=out_specs, out_shape=out_shape,
        compiler_params=_cparams(("parallel",)), name="norm_matmul",
    )(x, g.reshape(1, d), *ws)


def _peer_query_kernel(x_ref, g_ref, w_ref, q_ref, h_ref):
    hb = _rms(x_ref[...], g_ref[...]).astype(BF16)
    h_ref[...] = hb
    for hd in range(PEER_HEADS):
        q = _dot(hb, w_ref[:, hd * 2 * PEER_HALF:(hd + 1) * 2 * PEER_HALF])
        q_ref[2 * hd] = q[:, :PEER_HALF]
        q_ref[2 * hd + 1] = q[:, PEER_HALF:]


def peer_query(x, g, w, *, tm=256):
    n, d = x.shape
    tm = min(tm, n)
    nq = 2 * PEER_HEADS
    return pl.pallas_call(
        _peer_query_kernel, grid=(n // tm,),
        in_specs=[pl.BlockSpec((tm, d), lambda i: (i, 0)), pl.BlockSpec((1, d), lambda i: (0, 0)),
                  pl.BlockSpec(w.shape, lambda i: (0, 0))],
        out_specs=[pl.BlockSpec((nq, tm, PEER_HALF), lambda i: (0, i, 0)),
                   pl.BlockSpec((tm, d), lambda i: (i, 0))],
        out_shape=[jax.ShapeDtypeStruct((nq, n, PEER_HALF), F32), jax.ShapeDtypeStruct((n, d), BF16)],
        compiler_params=_cparams(("parallel",)), name="peer_query",
    )(x, g.reshape(1, d), w)


def _pool_kernel(u_ref, buf_ref, w_ref, scale_ref, y_ref, ext_ref, *, tt, start):
    ti = pl.program_id(1)

    @pl.when(ti == 0)
    def _():
        ext_ref[0:POOL_HIST, :] = buf_ref[0]

    ext_ref[POOL_HIST:POOL_HIST + tt, :] = u_ref[0]
    pos = start + ti * tt + lax.broadcasted_iota(jnp.int32, (tt, 1), 0)
    for gi, win in enumerate(POOL_WINDOWS):
        sl = slice(gi * POOL_GW, (gi + 1) * POOL_GW)
        u = ext_ref[POOL_HIST:POOL_HIST + tt, sl]
        s = u
        for k in range(1, win):
            s = s + ext_ref[POOL_HIST - k:POOL_HIST - k + tt, sl]
        cnt = jnp.minimum(pos + 1, win).astype(F32)
        d = s / cnt - u
        y_ref[0, :, sl] = _dot(d.astype(BF16), w_ref[gi]) * scale_ref[:, sl]
    ext_ref[0:POOL_HIST, :] = ext_ref[tt:tt + POOL_HIST, :]


def pool_mixer(u, buf, w_grp, scale, start, *, tt=512):
    b, t, c = u.shape
    tt = min(tt, t)
    hist = jnp.pad(buf, ((0, 0), (POOL_HIST - POOL_BUF, 0), (0, 0)))
    return pl.pallas_call(
        functools.partial(_pool_kernel, tt=tt, start=start), grid=(b, t // tt),
        in_specs=[pl.BlockSpec((1, tt, c), lambda i, j: (i, j, 0)),
                  pl.BlockSpec((1, POOL_HIST, c), lambda i, j: (i, 0, 0)),
                  pl.BlockSpec(w_grp.shape, lambda i, j: (0, 0, 0)),
                  pl.BlockSpec((1, c), lambda i, j: (0, 0))],
        out_specs=pl.BlockSpec((1, tt, c), lambda i, j: (i, j, 0)),
        out_shape=jax.ShapeDtypeStruct((b, t, c), F32),
        scratch_shapes=[pltpu.VMEM((POOL_HIST + tt, c), F32)],
        compiler_params=_cparams(("parallel", "arbitrary")), name="pool_mixer",
    )(u, hist, w_grp, scale.reshape(1, c))


def _softplus(x):
    return jnp.maximum(x, 0.0) + jnp.log(1.0 + jnp.exp(-jnp.abs(x)))


def _conv_kernel(x_ref, buf_ref, w_ref, ba_ref, alog_ref, dtb_ref, q_ref, k_ref, v_ref, bg_ref,
                 ext_ref, *, tt):
    ti = pl.program_id(1)

    @pl.when(ti == 0)
    def _():
        ext_ref[0:CONV_HIST, :] = buf_ref[0]

    ext_ref[CONV_HIST:CONV_HIST + tt, :] = x_ref[0]
    base = CONV_HIST - (DN_CONV - 1)
    y = ext_ref[base:base + tt, :] * w_ref[0:1, :]
    for j in range(1, DN_CONV):
        y = y + ext_ref[base + j:base + j + tt, :] * w_ref[j:j + 1, :]
    y = y * _sigmoid(y)
    for part, o_ref in enumerate((q_ref, k_ref, v_ref)):
        for h in range(DN_HEADS):
            a = y[:, part * BRANCH_W + h * DN_HD:part * BRANCH_W + (h + 1) * DN_HD]
            if part < 2:
                a = a * lax.rsqrt(jnp.sum(a * a, axis=-1, keepdims=True) + EPS)
            o_ref[0, :, h * DN_HD:(h + 1) * DN_HD] = a
    ba = ba_ref[0]
    beta = _sigmoid(ba)
    g = -jnp.exp(alog_ref[...]) * _softplus(ba + dtb_ref[...])
    lane = lax.broadcasted_iota(jnp.int32, ba.shape, 1)
    bg_ref[0] = jnp.where(lane < DN_HEADS, beta, g)
    ext_ref[0:CONV_HIST, :] = ext_ref[tt:tt + CONV_HIST, :]


def conv_prep(qkv, buf, w_conv, ba, a_log, dt_bias, *, tt=256):
    b, t, c = qkv.shape
    tt = min(tt, t)
    hist = jnp.pad(buf, ((0, 0), (CONV_HIST - (DN_CONV - 1), 0), (0, 0)))
    pad = jnp.zeros((LANES - 2 * DN_HEADS,), F32)
    alog_v = jnp.concatenate([jnp.zeros((DN_HEADS,), F32), a_log, pad]).reshape(1, LANES)
    dtb_v = jnp.concatenate([jnp.zeros((DN_HEADS,), F32), dt_bias, pad]).reshape(1, LANES)
    seq = lambda w: pl.BlockSpec((1, tt, w), lambda i, j: (i, j, 0))
    return pl.pallas_call(
        functools.partial(_conv_kernel, tt=tt), grid=(b, t // tt),
        in_specs=[seq(c), pl.BlockSpec((1, CONV_HIST, c), lambda i, j: (i, 0, 0)),
                  pl.BlockSpec(w_conv.shape, lambda i, j: (0, 0)), seq(LANES),
                  pl.BlockSpec((1, LANES), lambda i, j: (0, 0)),
                  pl.BlockSpec((1, LANES), lambda i, j: (0, 0))],
        out_specs=[seq(BRANCH_W), seq(BRANCH_W), seq(BRANCH_W), seq(LANES)],
        out_shape=[jax.ShapeDtypeStruct((b, t, BRANCH_W), F32)] * 3
        + [jax.ShapeDtypeStruct((b, t, LANES), F32)],
        scratch_shapes=[pltpu.VMEM((CONV_HIST + tt, c), F32)],
        compiler_params=_cparams(("parallel", "arbitrary")), name="conv_prep",
    )(qkv, hist, w_conv, ba, alog_v, dtb_v)


def _delta_kernel(q_ref, k_ref, v_ref, bg_ref, bgt_ref, z_ref, s0_ref, gdn_ref, y_ref, sfin_ref,
                  s_ref, *, c):
    n = pl.program_id(1)

    @pl.when(n == 0)
    def _():
        s_ref[...] = s0_ref[0]

    row = lax.broadcasted_iota(jnp.int32, (c, c), 0)
    col = lax.broadcasted_iota(jnp.int32, (c, c), 1)
    ltri = (row >= col).astype(F32)
    bg = bg_ref[0]
    gc_all = _dot(ltri, bg, HIGHEST)
    gct_all = _dot_nt(bgt_ref[0, 0], ltri, HIGHEST)
    n_double = int(round(math.log2(c))) - 1
    shift = int(round(math.log2(c)))
    heads = range(DN_HEADS)
    hc = DN_HEADS * c

    def stack(x):
        return jnp.concatenate([x[:, h * DN_HD:(h + 1) * DN_HD] for h in heads], axis=0)

    def rows(x, h):
        return x[h * c:(h + 1) * c]

    brow = lax.broadcasted_iota(jnp.int32, (hc, hc), 0)
    bcol = lax.broadcasted_iota(jnp.int32, (hc, hc), 1)
    same = lax.shift_right_logical(brow, shift) == lax.shift_right_logical(bcol, shift)
    tril = same & (brow >= bcol)
    strict = same & (brow > bcol)
    eye = (brow == bcol).astype(F32)
    q = stack(q_ref[0]) * (DN_HD ** -0.5)
    k = stack(k_ref[0])
    v = stack(v_ref[0])
    beta = jnp.concatenate([bg[:, h:h + 1] for h in heads], axis=0)
    gc = jnp.concatenate([gc_all[:, DN_HEADS + h:DN_HEADS + h + 1] for h in heads], axis=0)
    gcr = jnp.concatenate([gct_all[DN_HEADS + h:DN_HEADS + h + 1, :] for h in heads], axis=1)
    gls = [gc_all[c - 1:c, DN_HEADS + h:DN_HEADS + h + 1] for h in heads]
    gl = jnp.concatenate([jnp.broadcast_to(g, (c, 1)) for g in gls], axis=0)
    decay = jnp.where(tril, jnp.exp(jnp.where(tril, gc - gcr, 0.0)), 0.0)
    kb = k * beta
    vb = v * beta
    k16 = k.astype(BF16)
    kq = _dot_nt(jnp.concatenate([kb.astype(BF16), q.astype(BF16)], axis=0), k16)
    a = jnp.where(strict, kq[:hc] * decay, 0.0)
    attn = jnp.where(tril, kq[hc:] * decay, 0.0)
    x = -a
    tinv = eye + x
    (p,) = _split_matmul([x], x)
    for m in range(n_double):
        if m + 1 < n_double:
            tp, p = _split_matmul([tinv, p], p)
        else:
            (tp,) = _split_matmul([tinv], p)
        tinv = tinv + tp
    egc = jnp.exp(gc)
    (uw,) = _split_matmul([tinv], jnp.concatenate([vb, kb * egc], axis=1))
    u = uw[:, :DN_HD]
    w16 = uw[:, DN_HD:].astype(BF16)
    qg16 = (q * egc).astype(BF16)
    kd16 = (k * jnp.exp(gl - gc)).astype(BF16)
    states = [s_ref[h] for h in heads]
    ws = [_dot(jnp.concatenate([rows(w16, h), rows(qg16, h)], axis=0), states[h].astype(BF16))
          for h in heads]
    v_new = u - jnp.concatenate([r[:c] for r in ws], axis=0)
    v16 = v_new.astype(BF16)
    o = jnp.concatenate([r[c:] for r in ws], axis=0) + _dot(attn.astype(BF16), v16)
    for h in heads:
        s_ref[h] = states[h] * jnp.exp(gls[h]) + _dot_tn(rows(kd16, h), rows(v16, h))
    o = o * lax.rsqrt(jnp.mean(o * o, axis=-1, keepdims=True) + EPS) * gdn_ref[...]
    z = stack(z_ref[0])
    o = o * (z * _sigmoid(z))
    for h in heads:
        y_ref[0, :, h * DN_HD:(h + 1) * DN_HD] = rows(o, h)

    @pl.when(n == pl.num_programs(1) - 1)
    def _():
        sfin_ref[0] = s_ref[...]


def gated_delta(q, k, v, bg, z, s0, g_dn_out, *, c):
    b, t, _ = q.shape
    nc = t // c
    bgt = jnp.swapaxes(bg[..., :SUBLANES].reshape(b, nc, c, SUBLANES), 2, 3)
    seq = lambda w: pl.BlockSpec((1, c, w), lambda i, j: (i, j, 0))
    st = pl.BlockSpec((1, DN_HEADS, DN_HD, DN_HD), lambda i, j: (i, 0, 0, 0))
    return pl.pallas_call(
        functools.partial(_delta_kernel, c=c), grid=(b, nc),
        in_specs=[seq(BRANCH_W), seq(BRANCH_W), seq(BRANCH_W), seq(LANES),
                  pl.BlockSpec((1, 1, SUBLANES, c), lambda i, j: (i, j, 0, 0)), seq(BRANCH_W), st,
                  pl.BlockSpec((1, DN_HD), lambda i, j: (0, 0))],
        out_specs=[seq(BRANCH_W), st],
        out_shape=[jax.ShapeDtypeStruct((b, t, BRANCH_W), F32),
                   jax.ShapeDtypeStruct((b, DN_HEADS, DN_HD, DN_HD), F32)],
        scratch_shapes=[pltpu.VMEM((DN_HEADS, DN_HD, DN_HD), F32)],
        compiler_params=_cparams(("parallel", "arbitrary")), name="gated_delta",
    )(q, k, v, bg, bgt, z, s0, g_dn_out.reshape(1, DN_HD))


def _attn_kernel(q_ref, k_ref, v_ref, o_ref):
    for h in range(XA_HEADS):
        sl = slice(h * XA_HD, (h + 1) * XA_HD)
        q = q_ref[0, :, sl].astype(BF16)
        k = k_ref[0, :, sl].astype(BF16)
        v = v_ref[0, :, sl].astype(BF16)
        s = _dot_nt(q, k) * (XA_HD ** -0.5)
        p = jnp.exp(s - jnp.max(s, axis=-1, keepdims=True))
        p = p / jnp.sum(p, axis=-1, keepdims=True)
        o_ref[0, :, sl] = _dot(p.astype(BF16), v)


def mem_attend(xq, mk, mv, *, tq=512):
    b, t, c = xq.shape
    tq = min(tq, t)
    kv = pl.BlockSpec((1, N_MEM, c), lambda i, j: (i, 0, 0))
    return pl.pallas_call(
        _attn_kernel, grid=(b, t // tq),
        in_specs=[pl.BlockSpec((1, tq, c), lambda i, j: (i, j, 0)), kv, kv],
        out_specs=pl.BlockSpec((1, tq, c), lambda i, j: (i, j, 0)),
        out_shape=jax.ShapeDtypeStruct((b, t, c), F32),
        compiler_params=_cparams(("parallel", "parallel")), name="mem_attend",
    )(xq, mk, mv)


def _merge_kernel(x_ref, yp_ref, yd_ref, ym_ref, gate_ref, wb_ref, wo_ref, o_ref):
    acc = None
    for n, y_ref in enumerate((yp_ref, yd_ref, ym_ref)):
        bp = _dot(y_ref[...].astype(BF16), wb_ref[n])
        t = _sigmoid(gate_ref[:, n * D_MODEL:(n + 1) * D_MODEL]) * bp
        acc = t if acc is None else acc + t
    o_ref[...] = x_ref[...] + _dot(acc.astype(BF16), wo_ref[...])


def merge(x, y_pool, y_dn, y_mem, gate_raw, w_branch, w_o, *, tm=256):
    n, d = x.shape
    tm = min(tm, n)
    row = lambda w: pl.BlockSpec((tm, w), lambda i: (i, 0))
    return pl.pallas_call(
        _merge_kernel, grid=(n // tm,),
        in_specs=[row(d), row(BRANCH_W), row(BRANCH_W), row(BRANCH_W), row(3 * d),
                  pl.BlockSpec(w_branch.shape, lambda i: (0, 0, 0)),
                  pl.BlockSpec(w_o.shape, lambda i: (0, 0))],
        out_specs=row(d), out_shape=jax.ShapeDtypeStruct((n, d), F32),
        compiler_params=_cparams(("parallel",)), name="merge",
    )(x, y_pool, y_dn, y_mem, gate_raw, w_branch, w_o)


NEG_INF = float("-inf")


def _top_values(x, k):
    vals = []
    for r in range(k):
        m = jnp.max(x, axis=0, keepdims=True)
        vals.append(m)
        if r + 1 < k:
            x = jnp.where(x == m, NEG_INF, x)
    return vals


def _peer_topk_kernel(q_ref, sub_ref, s2_ref, p2_ref, c1_ref, e1_ref):
    def head(hd, carry):
        s1 = _dot_nt(sub_ref[2 * hd], q_ref[2 * hd].astype(BF16))
        s2 = _dot_nt(sub_ref[2 * hd + 1], q_ref[2 * hd + 1].astype(BF16))
        t1 = _top_values(s1, PEER_TOPK)
        t2 = jnp.concatenate(_top_values(s2, PEER_TOPK), axis=0)
        half = SUBLANES
        t2s = [t2 if a == 0 else t2[:half] if a < half else t2[:1] for a in range(PEER_TOPK)]
        cands = [a + b for a, b in zip(t1, t2s)]
        best = _top_values(jnp.concatenate(cands, axis=0), PEER_TOPK)
        m1, m2 = t1[0], t2[0:1]
        zsum = jnp.ones_like(m1)
        for r in range(1, PEER_TOPK):
            zsum = zsum + jnp.exp(best[r] - best[0])
        thr = best[PEER_TOPK - 1]
        c1 = jnp.full_like(s1, jnp.inf)
        for a, ca, b in zip(t1, cands, t2s):
            cut = jnp.min(jnp.where(ca >= thr, b, jnp.inf), axis=0, keepdims=True)
            c1 = jnp.where(s1 == a, cut, c1)
        s2_ref[hd] = s2
        p2_ref[hd] = jnp.exp(s2 - m2)
        c1_ref[hd] = c1
        e1_ref[hd] = jnp.exp(s1 - m1) / zsum
        return carry

    lax.fori_loop(0, PEER_HEADS, head, 0)


def peer_topk(qs, subkeys, *, tm=256):
    nq, n, c = qs.shape
    tm = min(tm, n)
    o_spec = pl.BlockSpec((PEER_HEADS, PEER_NKEYS, tm), lambda i: (0, 0, i))
    o_shape = jax.ShapeDtypeStruct((PEER_HEADS, PEER_NKEYS, n), F32)
    return pl.pallas_call(
        _peer_topk_kernel, grid=(n // tm,),
        in_specs=[pl.BlockSpec((nq, tm, c), lambda i: (0, i, 0)),
                  pl.BlockSpec(subkeys.shape, lambda i: (0, 0, 0))],
        out_specs=[o_spec] * 4, out_shape=[o_shape] * 4,
        compiler_params=_cparams(("parallel",)), name="peer_topk",
    )(qs, subkeys)


PEER_ROWS = 8
PEER_TE = PEER_ROWS * PEER_NKEYS


def _gelu_tanh(x):
    return 0.5 * x * (1.0 + jnp.tanh(math.sqrt(2.0 / math.pi) * (x + 0.044715 * (x * x * x))))


def _peer_dense_kernel(ht_ref, u_ref, vt_ref, s2_ref, p2_ref, c1_ref, e1_ref, x_ref, o_ref,
                       acc_ref, a_ref, w_ref):
    e = pl.program_id(1)

    @pl.when(e == 0)
    def _():
        acc_ref[...] = jnp.zeros_like(acc_ref)

    a_ref[...] = _dot(u_ref[...], ht_ref[...])

    def rows(r, carry):
        gate = None
        for hd in range(PEER_HEADS):
            c1 = c1_ref[hd, pl.ds(r, 1), :]
            e1 = e1_ref[hd, pl.ds(r, 1), :]
            t = jnp.where(s2_ref[hd] >= c1, p2_ref[hd], 0.0) * e1
            gate = t if gate is None else gate + t
        r0 = pl.multiple_of(r * PEER_NKEYS, PEER_NKEYS)
        act = _gelu_tanh(a_ref[pl.ds(r0, PEER_NKEYS), :])
        w_ref[pl.ds(r0, PEER_NKEYS), :] = (gate * act).astype(BF16)
        return carry

    lax.fori_loop(0, PEER_ROWS, rows, 0)
    acc_ref[...] += _dot(vt_ref[...], w_ref[...])

    @pl.when(e == pl.num_programs(1) - 1)
    def _():
        o_ref[...] = x_ref[...] + acc_ref[...].T


def peer_dense(ht, u, vt, s2, p2, c1, e1, x, *, tm=512):
    n, d = x.shape
    tm = min(tm, n)
    tok = pl.BlockSpec((PEER_HEADS, PEER_NKEYS, tm), lambda i, e: (0, 0, i))
    key = pl.BlockSpec((PEER_HEADS, PEER_ROWS, tm), lambda i, e: (0, e, i))
    return pl.pallas_call(
        _peer_dense_kernel, grid=(n // tm, PEER_N // PEER_TE),
        in_specs=[pl.BlockSpec((d, tm), lambda i, e: (0, i)),
                  pl.BlockSpec((PEER_TE, d), lambda i, e: (e, 0)),
                  pl.BlockSpec((d, PEER_TE), lambda i, e: (0, e)),
                  tok, tok, key, key,
                  pl.BlockSpec((tm, d), lambda i, e: (i, 0))],
        out_specs=pl.BlockSpec((tm, d), lambda i, e: (i, 0)),
        out_shape=jax.ShapeDtypeStruct((n, d), F32),
        scratch_shapes=[pltpu.VMEM((d, tm), F32), pltpu.VMEM((PEER_TE, tm), F32),
                        pltpu.VMEM((PEER_TE, tm), BF16)],
        compiler_params=_cparams(("parallel", "arbitrary")), name="peer_dense",
    )(ht, u, vt, s2, p2, c1, e1, x)


def _final_norm_kernel(x_ref, g_ref, o_ref):
    o_ref[...] = _rms(x_ref[...], g_ref[...])


def final_norm(x, g, *, tm=512):
    n, d = x.shape
    tm = min(tm, n)
    return pl.pallas_call(
        _final_norm_kernel, grid=(n // tm,),
        in_specs=[pl.BlockSpec((tm, d), lambda i: (i, 0)), pl.BlockSpec((1, d), lambda i: (0, 0))],
        out_specs=pl.BlockSpec((tm, d), lambda i: (i, 0)),
        out_shape=jax.ShapeDtypeStruct((n, d), F32),
        compiler_params=_cparams(("parallel",)), name="final_norm",
    )(x, g.reshape(1, d))


def _layer_weights(l, g_mix, w_in, w_conv, a_log, dt_bias, g_dn_out, w_pool_grp, pool_scale, g_mem,
                   w_mem_kv, w_branch, w_o, g_ffn, w_peer_q, peer_subkeys, peer_u, peer_v):
    wi = w_in[l]
    ba_cols = jnp.pad(wi[:, OFF_BETA:OFF_XQ], ((0, 0), (0, LANES - 2 * DN_HEADS)))
    w_in_parts = [wi[:, :OFF_Q], wi[:, OFF_Q:OFF_Z], wi[:, OFF_Z:OFF_BETA], wi[:, OFF_XQ:OFF_GATE],
                  wi[:, OFF_GATE:], ba_cols]
    return dict(
        g_mix=g_mix[l], w_in=[w.astype(BF16) for w in w_in_parts], w_conv=w_conv[l], a_log=a_log[l],
        dt_bias=dt_bias[l], g_dn_out=g_dn_out[l], w_pool=w_pool_grp[l].astype(BF16),
        pool_scale=pool_scale[l], g_mem=g_mem[l],
        w_mem=[w_mem_kv[l][:, :BRANCH_W].astype(BF16), w_mem_kv[l][:, BRANCH_W:].astype(BF16)],
        w_branch=w_branch[l].astype(BF16), w_o=w_o[l].astype(BF16), g_ffn=g_ffn[l],
        w_peer_q=w_peer_q[l].astype(BF16),
        subkeys=peer_subkeys[l].reshape(2 * PEER_HEADS, PEER_NKEYS, PEER_HALF).astype(BF16),
        peer_u=peer_u[l].astype(BF16), peer_vt=peer_v[l].T.astype(BF16))


def _layer(x, start, mk, mv, pool_buf, conv_buf, s0, wl):
    b, t, d = x.shape
    n = b * t
    xf = x.reshape(n, d)
    u_pool, qkv, z, xq, gate_raw, ba = norm_matmul(xf, wl["g_mix"], wl["w_in"])
    u_pool = u_pool.reshape(b, t, BRANCH_W)
    qkv = qkv.reshape(b, t, 3 * BRANCH_W)
    y_pool = pool_mixer(u_pool, pool_buf, wl["w_pool"], wl["pool_scale"], start)
    q, k, v, bg = conv_prep(qkv, conv_buf, wl["w_conv"], ba.reshape(b, t, LANES), wl["a_log"],
                            wl["dt_bias"])
    c = min(DN_CHUNK, -(-t // BF16_ROWS) * BF16_ROWS)
    tp = -(-t // c) * c
    z3 = z.reshape(b, t, BRANCH_W)
    if tp != t:
        padt = lambda a: jnp.pad(a, ((0, 0), (0, tp - t), (0, 0)))
        q, k, v, bg, z3 = (padt(a) for a in (q, k, v, bg, z3))
    y_dn, s_new = gated_delta(q, k, v, bg, z3, s0, wl["g_dn_out"], c=c)
    y_dn = y_dn[:, :t]
    y_mem = mem_attend(xq.reshape(b, t, BRANCH_W), mk, mv)
    x2 = merge(xf, y_pool.reshape(n, BRANCH_W), y_dn.reshape(n, BRANCH_W),
               y_mem.reshape(n, BRANCH_W), gate_raw, wl["w_branch"], wl["w_o"])
    qs, hb = peer_query(x2, wl["g_ffn"], wl["w_peer_q"])
    s2, p2, c1, e1 = peer_topk(qs, wl["subkeys"])
    x3 = peer_dense(hb.T, wl["peer_u"], wl["peer_vt"], s2, p2, c1, e1, x2)
    new_pool = jnp.concatenate([pool_buf, u_pool], axis=1)[:, -POOL_BUF:]
    new_conv = jnp.concatenate([conv_buf, qkv], axis=1)[:, -(DN_CONV - 1):]
    return x3.reshape(b, t, d), new_pool, new_conv, s_new


def kernel(x_prompt, x_sample, state_pool, state_conv, state_delta, cache_mem_k, cache_mem_v, mem_prompt, g_mix, w_in, w_conv, a_log, dt_bias, g_dn_out, w_pool_grp, pool_scale, g_mem, w_mem_kv, w_branch, w_o, g_ffn, w_peer_q, peer_subkeys, peer_u, peer_v, g_final):
    bp = x_prompt.shape[0]
    bs = x_sample.shape[0]
    xp, xs = x_prompt, x_sample
    outs_p = [[] for _ in range(5)]
    outs_s = [[] for _ in range(3)]
    mem_flat = mem_prompt.reshape(bp * N_MEM, D_MODEL)
    for l in range(DEPTH):
        wl = _layer_weights(l, g_mix, w_in, w_conv, a_log, dt_bias, g_dn_out, w_pool_grp, pool_scale,
                            g_mem, w_mem_kv, w_branch, w_o, g_ffn, w_peer_q, peer_subkeys, peer_u,
                            peer_v)
        mk, mv = norm_matmul(mem_flat, wl["g_mem"], wl["w_mem"])
        mk = mk.reshape(bp, N_MEM, BRANCH_W)
        mv = mv.reshape(bp, N_MEM, BRANCH_W)
        xp, pool_p, conv_p, delta_p = _layer(
            xp, 0, mk, mv, jnp.zeros((bp, POOL_BUF, BRANCH_W), F32),
            jnp.zeros((bp, DN_CONV - 1, 3 * BRANCH_W), F32),
            jnp.zeros((bp, DN_HEADS, DN_HD, DN_HD), F32), wl)
        for lst, a in zip(outs_p, (pool_p, conv_p, delta_p,
                                   mk.reshape(bp, N_MEM, XA_HEADS, XA_HD),
                                   mv.reshape(bp, N_MEM, XA_HEADS, XA_HD))):
            lst.append(a)
        xs, pool_s, conv_s, delta_s = _layer(
            xs, PAST_LEN, cache_mem_k[l].reshape(bs, N_MEM, BRANCH_W),
            cache_mem_v[l].reshape(bs, N_MEM, BRANCH_W), state_pool[l], state_conv[l],
            state_delta[l], wl)
        for lst, a in zip(outs_s, (pool_s, conv_s, delta_s)):
            lst.append(a)
    y_prompt = final_norm(xp.reshape(-1, D_MODEL), g_final).reshape(xp.shape)
    y_sample = final_norm(xs.reshape(-1, D_MODEL), g_final).reshape(xs.shape)
    return (y_prompt, y_sample) + tuple(jnp.stack(o, axis=0) for o in outs_p) \
        + tuple(jnp.stack(o, axis=0) for o in outs_s)
```
